```python
import jax, jax.numpy as jnp
from jax import lax
import numpy as np

D_MODEL = 1024
BATCH = 2
SEQ = 16384
DEPTH = 2

HEAD_DIM = 64
HEADS_PER_GROUP = 4
ATTN_GROUPS = ((128, 1), (512, 4), (2048, 16))
N_GROUPS = len(ATTN_GROUPS)
ATTN_HEADS = HEADS_PER_GROUP * N_GROUPS
QKV_W = ATTN_HEADS * HEAD_DIM
ATTN_OUT_W = HEADS_PER_GROUP * HEAD_DIM
CONV_W = D_MODEL
CONV_K = 31
D_FF = 2816
ROPE_THETA = 10000.0
EPS = 1e-6
N_BRANCH = 2
N_IN = 3 * QKV_W + 2 * CONV_W + N_BRANCH * D_MODEL
SPLITS = (QKV_W, 2 * QKV_W, 3 * QKV_W, 3 * QKV_W + 2 * CONV_W)
N_MOD = 9
NEG_INF = -1e30

kernel_name = "hybrid_dilated_attn_conformer_conv_macaron"


def rms_norm(x, g):
    xf = x.astype(jnp.float32)
    y = xf * lax.rsqrt(jnp.mean(xf * xf, axis=-1, keepdims=True) + EPS)
    return (y * g.astype(jnp.float32)).astype(x.dtype)


def layer_norm(x, g, b):
    xf = x.astype(jnp.float32)
    mu = jnp.mean(xf, axis=-1, keepdims=True)
    xc = xf - mu
    var = jnp.mean(xc * xc, axis=-1, keepdims=True)
    return (xc * lax.rsqrt(var + EPS) * g.astype(jnp.float32) + b.astype(jnp.float32)).astype(x.dtype)


def rope_tables(s):
    half = HEAD_DIM // 2
    inv_freq = ROPE_THETA ** (-(jnp.arange(half, dtype=jnp.float32) * 2.0 / HEAD_DIM))
    ang = jnp.arange(s, dtype=jnp.float32)[:, None] * inv_freq[None, :]
    return jnp.cos(ang), jnp.sin(ang)


def apply_rope(t, cos, sin):
    half = HEAD_DIM // 2
    tf = t.astype(jnp.float32)
    t1, t2 = tf[..., :half], tf[..., half:]
    c = cos[None, :, None, :]
    s = sin[None, :, None, :]
    return jnp.concatenate([t1 * c - t2 * s, t2 * c + t1 * s], axis=-1).astype(t.dtype)


def dilated_window_attention(q, k, v, window, dilation):
    b, s, h, e = q.shape
    w = window // dilation
    sp = -(-s // window) * window
    pad = sp - s
    nb = sp // window
    seg = sp // dilation

    def to_blocks(t):
        t = jnp.pad(t, ((0, 0), (0, pad), (0, 0), (0, 0)))
        t = t.reshape(b, seg, dilation, h, e).transpose(0, 2, 1, 3, 4)
        return t.reshape(b, dilation, nb, w, h, e)

    def with_prev(t):
        prev = jnp.pad(t, ((0, 0), (0, 0), (1, 0), (0, 0), (0, 0), (0, 0)))[:, :, :-1]
        return jnp.concatenate([prev, t], axis=3)

    qb = to_blocks(q)
    kk = with_prev(to_blocks(k))
    vv = with_prev(to_blocks(v))

    scores = jnp.einsum('bdnqhe,bdnkhe->bdnhqk', qb, kk,
                        preferred_element_type=jnp.float32) * (e ** -0.5)
    qi = jnp.arange(w)[:, None]
    kj = jnp.arange(2 * w)[None, :]
    dist = qi + w - kj
    band = (dist >= 0) & (dist <= w)
    has_prev = (jnp.arange(nb)[:, None, None] > 0) | (kj[None] >= w)
    mask = (band[None] & has_prev)[:, None]
    scores = jnp.where(mask, scores, NEG_INF)
    m = jnp.max(scores, axis=-1, keepdims=True)
    p = jnp.exp(scores - m)
    den = jnp.sum(p, axis=-1)
    o = jnp.einsum('bdnhqk,bdnkhe->bdnqhe', p, vv.astype(jnp.float32))
    o = o / jnp.transpose(den, (0, 1, 2, 4, 3))[..., None]
    lse = m[..., 0] + jnp.log(den)

    o = o.reshape(b, dilation, seg, h, e).transpose(0, 2, 1, 3, 4).reshape(b, sp, h, e)[:, :s]
    lse = jnp.transpose(lse, (0, 1, 2, 4, 3)).reshape(b, dilation, seg, h)
    lse = lse.transpose(0, 2, 1, 3).reshape(b, sp, h)[:, :s]
    return o, lse


def conv_module(u, conv_w, conv_b, ln_g, ln_b, w_o):
    a, g = jnp.split(u, 2, axis=-1)
    h = a * jax.nn.sigmoid(g)
    h = lax.conv_general_dilated(h, conv_w[:, None, :], window_strides=(1,),
                                 padding=[(CONV_K - 1, 0)],
                                 dimension_numbers=('NWC', 'WIO', 'NWC'),
                                 feature_group_count=CONV_W) + conv_b
    h = jax.nn.silu(layer_norm(h, ln_g, ln_b))
    return h @ w_o


def swiglu(h, wg, wu, wd):
    return (jax.nn.silu(h @ wg) * (h @ wu)) @ wd


def setup_inputs(seed: int = 0) -> dict:
    key = jax.random.key(seed)
    ks = jax.random.split(key, 20)
    f32 = jnp.float32
    nrm = lambda k, shape, scale: (jax.random.normal(k, shape, f32) * scale)
    return {
        "x": nrm(ks[0], (BATCH, SEQ, D_MODEL), 1.0),
        "c": nrm(ks[1], (BATCH, D_MODEL), 1.0),
        "ada_w": nrm(ks[2], (DEPTH, D_MODEL, N_MOD * D_MODEL), 0.5 * D_MODEL ** -0.5),
        "ada_b": nrm(ks[3], (DEPTH, N_MOD * D_MODEL), 0.02),
        "norm_g": 1.0 + nrm(ks[4], (DEPTH, 3, D_MODEL), 0.02),
        "ffn_wg": nrm(ks[5], (DEPTH, 2, D_MODEL, D_FF), D_MODEL ** -0.5),
        "ffn_wu": nrm(ks[6], (DEPTH, 2, D_MODEL, D_FF), D_MODEL ** -0.5),
        "ffn_wd": nrm(ks[7], (DEPTH, 2, D_FF, D_MODEL), D_FF ** -0.5),
        "w_in": nrm(ks[8], (DEPTH, D_MODEL, N_IN), D_MODEL ** -0.5),
        "attn_wo": nrm(ks[9], (DEPTH, ATTN_OUT_W, D_MODEL), ATTN_OUT_W ** -0.5),
        "conv_w": nrm(ks[10], (DEPTH, CONV_K, CONV_W), CONV_K ** -0.5),
        "conv_b": nrm(ks[11], (DEPTH, CONV_W), 0.02),
        "conv_ln_g": 1.0 + nrm(ks[12], (DEPTH, CONV_W), 0.02),
        "conv_ln_b": nrm(ks[13], (DEPTH, CONV_W), 0.02),
        "conv_wo": nrm(ks[14], (DEPTH, CONV_W, D_MODEL), CONV_W ** -0.5),
        "w_out": nrm(ks[15], (DEPTH, D_MODEL, D_MODEL), D_MODEL ** -0.5),
        "final_g": 1.0 + nrm(ks[16], (D_MODEL,), 0.02),
    }


def reference(x, c, ada_w, ada_b, norm_g, ffn_wg, ffn_wu, ffn_wd, w_in, attn_wo,
              conv_w, conv_b, conv_ln_g, conv_ln_b, conv_wo, w_out, final_g):
    b, s, d = x.shape
    cos, sin = rope_tables(s)
    c_act = jax.nn.silu(c)
    for l in range(DEPTH):
        mod = (c_act @ ada_w[l] + ada_b[l]).reshape(b, N_MOD, d)[:, :, None, :]
        shift = lambda i: mod[:, 3 * i]
        scale = lambda i: mod[:, 3 * i + 1]
        gate = lambda i: mod[:, 3 * i + 2]
        modulate = lambda t, i: rms_norm(t, norm_g[l, i]) * (1.0 + scale(i)) + shift(i)

        h = modulate(x, 0)
        x = x + 0.5 * gate(0) * swiglu(h, ffn_wg[l, 0], ffn_wu[l, 0], ffn_wd[l, 0])

        h = modulate(x, 1)
        z = h @ w_in[l]
        q, k, v, u, gates = jnp.split(z, SPLITS, axis=-1)
        q = apply_rope(q.reshape(b, s, ATTN_HEADS, HEAD_DIM), cos, sin)
        k = apply_rope(k.reshape(b, s, ATTN_HEADS, HEAD_DIM), cos, sin)
        v = v.reshape(b, s, ATTN_HEADS, HEAD_DIM)
        outs, lses = [], []
        for g_idx, (win, dil) in enumerate(ATTN_GROUPS):
            sl = slice(g_idx * HEADS_PER_GROUP, (g_idx + 1) * HEADS_PER_GROUP)
            o_g, lse_g = dilated_window_attention(q[:, :, sl], k[:, :, sl], v[:, :, sl], win, dil)
            outs.append(o_g)
            lses.append(lse_g)
        wts = jax.nn.softmax(jnp.stack(lses, axis=0), axis=0)
        o = jnp.sum(wts[..., None] * jnp.stack(outs, axis=0), axis=0)
        y_attn = o.astype(x.dtype).reshape(b, s, ATTN_OUT_W) @ attn_wo[l]
        y_conv = conv_module(u, conv_w[l], conv_b[l], conv_ln_g[l], conv_ln_b[l], conv_wo[l])
        g_attn, g_conv = jnp.split(gates, N_BRANCH, axis=-1)
        y = jax.nn.sigmoid(g_attn) * y_attn + jax.nn.sigmoid(g_conv) * y_conv
        x = x + gate(1) * (y @ w_out[l])

        h = modulate(x, 2)
        x = x + 0.5 * gate(2) * swiglu(h, ffn_wg[l, 1], ffn_wu[l, 1], ffn_wd[l, 1])
    return rms_norm(x, final_g)
```

```python
import functools

import jax
import jax.numpy as jnp
from jax import lax
from jax.experimental import pallas as pl
from jax.experimental.pallas import tpu as pltpu

D_MODEL = 1024
DEPTH = 2
HEAD_DIM = 64
HEADS_PER_GROUP = 4
ATTN_GROUPS = ((128, 1), (512, 4), (2048, 16))
N_GROUPS = len(ATTN_GROUPS)
GROUP_W = HEADS_PER_GROUP * HEAD_DIM
QKV_W = N_GROUPS * GROUP_W
CONV_K = 31
D_FF = 2816
ROPE_THETA = 10000.0
EPS = 1e-6
N_MOD = 9
NEG_INF = -1e30
N_IN = 3 * QKV_W + 2 * D_MODEL + 2 * D_MODEL

LANES = 128
BLK = 128
HALO = 32
VMEM_LIMIT = 56 * 1024 * 1024

F32 = jnp.float32
BF16 = jnp.bfloat16


def _const_spec(shape):
    nd = len(shape)
    return pl.BlockSpec(shape, lambda *_: (0,) * nd, pipeline_mode=pl.Buffered(1))


def _params(n_axes):
    return pltpu.CompilerParams(
        dimension_semantics=("arbitrary",) * n_axes, vmem_limit_bytes=VMEM_LIMIT)


def _sigmoid(v):
    return 1.0 / (1.0 + jnp.exp(-v))


def _rms_norm(x, g):
    return (x * lax.rsqrt(jnp.mean(x * x, axis=-1, keepdims=True) + EPS)) * g


def _modulate(x, g, mod_ref, i):
    shift = mod_ref[0, 3 * i:3 * i + 1, :]
    scale = mod_ref[0, 3 * i + 1:3 * i + 2, :]
    return _rms_norm(x, g) * (1.0 + scale) + shift


def _ada_kernel(c_ref, w_ref, b_ref, o_ref):
    c = c_ref[...]
    ca = c * _sigmoid(c)
    o_ref[0] = jnp.dot(ca, w_ref[0], preferred_element_type=F32,
                       precision=lax.Precision.HIGHEST) + b_ref[0]


def _ada(c, ada_w, ada_b):
    b = c.shape[0]
    rows = 8
    c_pad = jnp.pad(c, ((0, rows - b), (0, 0)))
    n = N_MOD * D_MODEL
    tn = D_MODEL
    out = pl.pallas_call(
        _ada_kernel,
        grid=(DEPTH, n // tn),
        in_specs=[
            pl.BlockSpec((rows, D_MODEL), lambda l, j: (0, 0)),
            pl.BlockSpec((1, D_MODEL, tn), lambda l, j: (l, 0, j)),
            pl.BlockSpec((1, 1, tn), lambda l, j: (l, 0, j)),
        ],
        out_specs=pl.BlockSpec((1, rows, tn), lambda l, j: (l, 0, j)),
        out_shape=jax.ShapeDtypeStruct((DEPTH, rows, n), F32),
        compiler_params=_params(2),
        name="ada",
    )(c_pad, ada_w, ada_b.reshape(DEPTH, 1, n))
    return out[:, :b].reshape(DEPTH, b, N_MOD, D_MODEL)


FFN_T = 512
FFN_CH = 256


def _ffn_kernel(x_ref, mod_ref, g_ref, wg_ref, wu_ref, wd_ref, fg_ref, o_ref, a_scr,
                *, idx, final):
    x = x_ref[0]
    h = _modulate(x, g_ref[...], mod_ref, idx).astype(BF16)
    for c in range(D_FF // FFN_CH):
        sl = slice(c * FFN_CH, (c + 1) * FFN_CH)
        gg = jnp.dot(h, wg_ref[:, sl], preferred_element_type=F32)
        uu = jnp.dot(h, wu_ref[:, sl], preferred_element_type=F32)
        a_scr[:, sl] = ((gg * _sigmoid(gg)) * uu).astype(BF16)
    y = jnp.dot(a_scr[...], wd_ref[...], preferred_element_type=F32)
    gate = mod_ref[0, 3 * idx + 2:3 * idx + 3, :]
    out = x + (0.5 * gate) * y
    if final:
        out = _rms_norm(out, fg_ref[...])
    o_ref[0] = out


def _ffn(x, mod, g, wg, wu, wd, fg, *, idx, final):
    b, s, d = x.shape
    t = FFN_T
    return pl.pallas_call(
        functools.partial(_ffn_kernel, idx=idx, final=final),
        grid=(b, s // t),
        in_specs=[
            pl.BlockSpec((1, t, d), lambda i, j: (i, j, 0)),
            pl.BlockSpec((1, N_MOD, d), lambda i, j: (i, 0, 0)),
            _const_spec((1, d)),
            _const_spec((d, D_FF)),
            _const_spec((d, D_FF)),
            _const_spec((D_FF, d)),
            _const_spec((1, d)),
        ],
        out_specs=pl.BlockSpec((1, t, d), lambda i, j: (i, j, 0)),
        out_shape=jax.ShapeDtypeStruct((b, s, d), F32),
        scratch_shapes=[pltpu.VMEM((t, D_FF), BF16)],
        compiler_params=_params(2),
        name="ffn_final" if final else "ffn",
    )(x, mod, g, wg, wu, wd, fg)


PRE_T = 512
PRE_CH = 256


def _pre_kernel(x_ref, mod_ref, g_ref, w_ref, cos_ref, sin_ref,
                q_ref, k_ref, v_ref, hg_ref, ga_ref, gc_ref):
    x = x_ref[0]
    h = _modulate(x, g_ref[...], mod_ref, 1).astype(BF16)
    cos = cos_ref[...]
    sin = sin_ref[...]
    t = x.shape[0]
    lane = lax.broadcasted_iota(jnp.int32, (t, LANES), 1)
    low_half = (lane % HEAD_DIM) < (HEAD_DIM // 2)

    def proj(c0):
        return jnp.dot(h, w_ref[:, c0:c0 + PRE_CH], preferred_element_type=F32)

    def rope(z):
        rot = jnp.where(low_half,
                        pltpu.roll(z, LANES - HEAD_DIM // 2, 1),
                        pltpu.roll(z, HEAD_DIM // 2, 1))
        return z * cos + rot * sin

    scale = HEAD_DIM ** -0.5
    for c in range(QKV_W // PRE_CH):
        zq = proj(c * PRE_CH)
        zk = proj(QKV_W + c * PRE_CH)
        zv = proj(2 * QKV_W + c * PRE_CH)
        for hlf in range(PRE_CH // LANES):
            sl = slice(hlf * LANES, (hlf + 1) * LANES)
            osl = slice(c * PRE_CH + hlf * LANES, c * PRE_CH + (hlf + 1) * LANES)
            q_ref[0, :, osl] = (rope(zq[:, sl]) * scale).astype(BF16)
            k_ref[0, :, osl] = rope(zk[:, sl]).astype(BF16)
        v_ref[0, :, c * PRE_CH:(c + 1) * PRE_CH] = zv.astype(BF16)
    u0 = 3 * QKV_W
    for c in range(D_MODEL // PRE_CH):
        za = proj(u0 + c * PRE_CH)
        zg = proj(u0 + D_MODEL + c * PRE_CH)
        hg_ref[0, :, c * PRE_CH:(c + 1) * PRE_CH] = za * _sigmoid(zg)
    g0 = u0 + 2 * D_MODEL
    for c in range(D_MODEL // PRE_CH):
        ga_ref[0, :, c * PRE_CH:(c + 1) * PRE_CH] = _sigmoid(proj(g0 + c * PRE_CH))
        gc_ref[0, :, c * PRE_CH:(c + 1) * PRE_CH] = _sigmoid(proj(g0 + D_MODEL + c * PRE_CH))


def _pre(x, mod, g, w_in, cos, sin):
    b, s, d = x.shape
    t = PRE_T
    tok = lambda w: pl.BlockSpec((1, t, w), lambda i, j: (i, j, 0))
    return pl.pallas_call(
        _pre_kernel,
        grid=(b, s // t),
        in_specs=[
            tok(d),
            pl.BlockSpec((1, N_MOD, d), lambda i, j: (i, 0, 0)),
            _const_spec((1, d)),
            _const_spec((d, N_IN)),
            pl.BlockSpec((t, LANES), lambda i, j: (j, 0)),
            pl.BlockSpec((t, LANES), lambda i, j: (j, 0)),
        ],
        out_specs=[tok(QKV_W), tok(QKV_W), tok(QKV_W), tok(d), tok(d), tok(d)],
        out_shape=[
            jax.ShapeDtypeStruct((b, s, QKV_W), BF16),
            jax.ShapeDtypeStruct((b, s, QKV_W), BF16),
            jax.ShapeDtypeStruct((b, s, QKV_W), BF16),
            jax.ShapeDtypeStruct((b, s, d), F32),
            jax.ShapeDtypeStruct((b, s, d), F32),
            jax.ShapeDtypeStruct((b, s, d), F32),
        ],
        compiler_params=_params(2),
        name="pre",
    )(x, mod, g, w_in, cos, sin)


ATTN_T = 512


def _attn_kernel(q_ref, kc_ref, kp_ref, vc_ref, vp_ref, o_ref, l_ref):
    n = pl.program_id(2)
    qi = lax.broadcasted_iota(jnp.int32, (BLK, 2 * BLK), 0)
    kj = lax.broadcasted_iota(jnp.int32, (BLK, 2 * BLK), 1)
    band = (kj >= qi) & (kj <= qi + BLK)
    band_first = band & (kj >= jnp.where(n == 0, BLK, 0))
    lane = lax.broadcasted_iota(jnp.int32, (BLK, LANES), 1)
    head0 = lane < HEAD_DIM
    zero = jnp.zeros((BLK, LANES), BF16)
    for i in range(ATTN_T // BLK):
        rows = slice(i * BLK, (i + 1) * BLK)
        q2 = q_ref[0, rows, :]
        if i == 0:
            k_prev, v_prev = kp_ref[0], vp_ref[0]
            mask = band_first
        else:
            prev = slice((i - 1) * BLK, i * BLK)
            k_prev, v_prev = kc_ref[0, prev, :], vc_ref[0, prev, :]
            mask = band
        kk = jnp.concatenate([k_prev, kc_ref[0, rows, :]], axis=0)
        vv = jnp.concatenate([v_prev, vc_ref[0, rows, :]], axis=0)
        for pair in range(GROUP_W // LANES):
            cols = slice(pair * LANES, (pair + 1) * LANES)
            qp, kp2, vp2 = q2[:, cols], kk[:, cols], vv[:, cols]
            outs, lses = [], []
            for hh in range(2):
                sel = head0 if hh == 0 else jnp.logical_not(head0)
                qh = jnp.where(sel, qp, zero)
                sc = lax.dot_general(qh, kp2, (((1,), (1,)), ((), ())),
                                     preferred_element_type=F32)
                sc = jnp.where(mask, sc, NEG_INF)
                mx = jnp.max(sc, axis=-1, keepdims=True)
                p = jnp.exp(sc - mx)
                den = jnp.sum(p, axis=-1, keepdims=True)
                oh = jnp.dot(p.astype(BF16), vp2, preferred_element_type=F32)
                outs.append(oh / den)
                lses.append(jnp.broadcast_to(mx + jnp.log(den), (BLK, LANES)))
            o_ref[0, rows, cols] = jnp.where(head0, outs[0], outs[1])
            l_ref[0, rows, cols] = jnp.where(head0, lses[0], lses[1])


def _attn_group(q, k, v, g_idx):
    b, s, _ = q.shape
    _, dil = ATTN_GROUPS[g_idx]
    sd = s // dil
    t = ATTN_T
    m = t // BLK
    ncol = QKV_W // GROUP_W
    view = lambda a: a.reshape(b, sd, dil * QKV_W)
    cur = pl.BlockSpec((1, t, GROUP_W), lambda i, r, n: (i, n, r * ncol + g_idx))
    prev = pl.BlockSpec((1, BLK, GROUP_W),
                        lambda i, r, n: (i, jnp.maximum(n * m - 1, 0), r * ncol + g_idx))
    out = pl.BlockSpec((1, t, GROUP_W), lambda i, r, n: (i, n, r))
    o, l = pl.pallas_call(
        _attn_kernel,
        grid=(b, dil, sd // t),
        in_specs=[cur, cur, prev, cur, prev],
        out_specs=[out, out],
        out_shape=[jax.ShapeDtypeStruct((b, sd, dil * GROUP_W), F32)] * 2,
        compiler_params=_params(3),
        name=f"attn_g{g_idx}",
    )(view(q), view(k), view(k), view(v), view(v))
    return o.reshape(b, s, GROUP_W), l.reshape(b, s, GROUP_W)


POST_T = 256
CONV_ROWS = 64


def _post_kernel(x_ref, mod_ref, hc_ref, hp_ref, ga_ref, gc_ref,
                 o0_ref, l0_ref, o1_ref, l1_ref, o2_ref, l2_ref,
                 cw_ref, cb_ref, lng_ref, lnb_ref, cwo_ref, awo_ref, wout_ref,
                 out_ref, hbuf, conv_scr):
    j = pl.program_id(1)
    t = POST_T
    hbuf[0:HALO, :] = jnp.where(j > 0, hp_ref[0], 0.0)
    hbuf[HALO:, :] = hc_ref[0]
    first = HALO - (CONV_K - 1)
    for c in range(D_MODEL // LANES):
        cols = slice(c * LANES, (c + 1) * LANES)
        w = cw_ref[:, cols]
        bias = cb_ref[:, cols]
        for rb in range(t // CONV_ROWS):
            r0 = rb * CONV_ROWS
            acc = jnp.broadcast_to(bias, (CONV_ROWS, LANES))
            for tap in range(CONV_K):
                acc = acc + hbuf[r0 + first + tap:r0 + first + tap + CONV_ROWS, cols] * w[tap:tap + 1, :]
            conv_scr[r0:r0 + CONV_ROWS, cols] = acc
    hv = conv_scr[...]
    mu = jnp.mean(hv, axis=-1, keepdims=True)
    xc = hv - mu
    var = jnp.mean(xc * xc, axis=-1, keepdims=True)
    hn = xc * lax.rsqrt(var + EPS) * lng_ref[...] + lnb_ref[...]
    hn = hn * _sigmoid(hn)
    y_conv = jnp.dot(hn.astype(BF16), cwo_ref[...], preferred_element_type=F32)

    l0, l1, l2 = l0_ref[0], l1_ref[0], l2_ref[0]
    lmax = jnp.maximum(jnp.maximum(l0, l1), l2)
    e0, e1, e2 = jnp.exp(l0 - lmax), jnp.exp(l1 - lmax), jnp.exp(l2 - lmax)
    o = (e0 * o0_ref[0] + e1 * o1_ref[0] + e2 * o2_ref[0]) / (e0 + e1 + e2)
    y_attn = jnp.dot(o.astype(BF16), awo_ref[...], preferred_element_type=F32)

    y = ga_ref[0] * y_attn + gc_ref[0] * y_conv
    gate = mod_ref[0, 5:6, :]
    out_ref[0] = x_ref[0] + gate * jnp.dot(y.astype(BF16), wout_ref[...],
                                           preferred_element_type=F32)


def _post(x, mod, hglu, ga, gc, attn, conv_w, conv_b, ln_g, ln_b, conv_wo, attn_wo, w_out):
    b, s, d = x.shape
    t = POST_T
    tok = lambda w: pl.BlockSpec((1, t, w), lambda i, j: (i, j, 0))
    halo = pl.BlockSpec((1, HALO, d),
                        lambda i, j: (i, jnp.maximum(j * (t // HALO) - 1, 0), 0))
    attn_specs = [tok(GROUP_W)] * (2 * N_GROUPS)
    return pl.pallas_call(
        _post_kernel,
        grid=(b, s // t),
        in_specs=[tok(d), pl.BlockSpec((1, N_MOD, d), lambda i, j: (i, 0, 0)),
                  tok(d), halo, tok(d), tok(d)] + attn_specs + [
            _const_spec((CONV_K, d)), _const_spec((1, d)), _const_spec((1, d)),
            _const_spec((1, d)), _const_spec((d, d)), _const_spec((GROUP_W, d)),
            _const_spec((d, d))],
        out_specs=tok(d),
        out_shape=jax.ShapeDtypeStruct((b, s, d), F32),
        scratch_shapes=[pltpu.VMEM((HALO + t, d), F32), pltpu.VMEM((t, d), F32)],
        compiler_params=_params(2),
        name="post",
    )(x, mod, hglu, hglu, ga, gc, *attn, conv_w, conv_b, ln_g, ln_b, conv_wo, attn_wo, w_out)


def _rope_tables(s):
    half = HEAD_DIM // 2
    inv_freq = ROPE_THETA ** (-(jnp.arange(half, dtype=F32) * 2.0 / HEAD_DIM))
    ang = jnp.arange(s, dtype=F32)[:, None] * inv_freq[None, :]
    cos, sin = jnp.cos(ang), jnp.sin(ang)
    reps = LANES // HEAD_DIM
    cos_t = jnp.tile(jnp.concatenate([cos, cos], axis=-1), (1, reps))
    sin_t = jnp.tile(jnp.concatenate([-sin, sin], axis=-1), (1, reps))
    return cos_t, sin_t


def kernel(x, c, ada_w, ada_b, norm_g, ffn_wg, ffn_wu, ffn_wd, w_in, attn_wo,
           conv_w, conv_b, conv_ln_g, conv_ln_b, conv_wo, w_out, final_g):
    b, s, d = x.shape
    cos_t, sin_t = _rope_tables(s)
    mod_all = _ada(c, ada_w, ada_b)
    row = lambda v: v.reshape(1, d)
    fg = row(final_g)
    for l in range(DEPTH):
        mod = mod_all[l]
        wg, wu, wd = (w[l].astype(BF16) for w in (ffn_wg, ffn_wu, ffn_wd))
        x = _ffn(x, mod, row(norm_g[l, 0]), wg[0], wu[0], wd[0], fg, idx=0, final=False)
        q, k, v, hglu, ga, gc = _pre(x, mod, row(norm_g[l, 1]), w_in[l].astype(BF16),
                                     cos_t, sin_t)
        attn = []
        for g_idx in range(N_GROUPS):
            attn.extend(_attn_group(q, k, v, g_idx))
        x = _post(x, mod, hglu, ga, gc, attn, conv_w[l], row(conv_b[l]),
                  row(conv_ln_g[l]), row(conv_ln_b[l]), conv_wo[l].astype(BF16),
                  attn_wo[l].astype(BF16), w_out[l].astype(BF16))
        x = _ffn(x, mod, row(norm_g[l, 2]), wg[1], wu[1], wd[1], fg, idx=2,
                 final=(l == DEPTH - 1))
    return x
```

```python
import functools

import jax
import jax.numpy as jnp
from jax import lax
from jax.experimental import pallas as pl
from jax.experimental.pallas import tpu as pltpu

D_MODEL = 1024
DEPTH = 2
HEAD_DIM = 64
HEADS_PER_GROUP = 4
ATTN_GROUPS = ((128, 1), (512, 4), (2048, 16))
N_GROUPS = len(ATTN_GROUPS)
GROUP_W = HEADS_PER_GROUP * HEAD_DIM
QKV_W = N_GROUPS * GROUP_W
CONV_K = 31
D_FF = 2816
ROPE_THETA = 10000.0
EPS = 1e-6
N_MOD = 9
NEG_INF = -1e30
N_IN = 3 * QKV_W + 2 * D_MODEL + 2 * D_MODEL

LANES = 128
SUBLANES = 8
BLK = 128
HALO = 32
CONV_ROWS = 64
VMEM_LIMIT = 56 * 1024 * 1024

F32 = jnp.float32
BF16 = jnp.bfloat16


def _const_spec(shape):
    nd = len(shape)
    return pl.BlockSpec(shape, lambda *_: (0,) * nd, pipeline_mode=pl.Buffered(1))


def _params(n_axes):
    return pltpu.CompilerParams(
        dimension_semantics=("arbitrary",) * n_axes, vmem_limit_bytes=VMEM_LIMIT)


def _sigmoid(v):
    return 1.0 / (1.0 + jnp.exp(-v))


def _rms_norm(x, g):
    return (x * lax.rsqrt(jnp.mean(x * x, axis=-1, keepdims=True) + EPS)) * g


def _modulate(x, g, mod_ref, i):
    shift = mod_ref[0, 3 * i:3 * i + 1, :]
    scale = mod_ref[0, 3 * i + 1:3 * i + 2, :]
    return _rms_norm(x, g) * (1.0 + scale) + shift


def _ada_kernel(c_ref, w_ref, b_ref, o_ref):
    c = c_ref[...]
    ca = c * _sigmoid(c)
    o_ref[0] = jnp.dot(ca, w_ref[0], preferred_element_type=F32,
                       precision=lax.Precision.HIGHEST) + b_ref[0]


def _ada(c, ada_w, ada_b):
    b = c.shape[0]
    rows = 8
    c_pad = jnp.pad(c, ((0, rows - b), (0, 0)))
    n = N_MOD * D_MODEL
    tn = D_MODEL
    out = pl.pallas_call(
        _ada_kernel,
        grid=(DEPTH, n // tn),
        in_specs=[
            pl.BlockSpec((rows, D_MODEL), lambda l, j: (0, 0)),
            pl.BlockSpec((1, D_MODEL, tn), lambda l, j: (l, 0, j)),
            pl.BlockSpec((1, 1, tn), lambda l, j: (l, 0, j)),
        ],
        out_specs=pl.BlockSpec((1, rows, tn), lambda l, j: (l, 0, j)),
        out_shape=jax.ShapeDtypeStruct((DEPTH, rows, n), F32),
        compiler_params=_params(2),
        name="ada",
    )(c_pad, ada_w, ada_b.reshape(DEPTH, 1, n))
    return out[:, :b].reshape(DEPTH, b, N_MOD, D_MODEL)


FFN_T = 512
FFN_CH = 256


def _ffn_kernel(x_ref, mod_ref, g_ref, wg_ref, wu_ref, wd_ref, fg_ref, o_ref, a_scr,
                *, idx, final):
    x = x_ref[0]
    h = _modulate(x, g_ref[...], mod_ref, idx).astype(BF16)
    for c in range(D_FF // FFN_CH):
        sl = slice(c * FFN_CH, (c + 1) * FFN_CH)
        gg = jnp.dot(h, wg_ref[:, sl], preferred_element_type=F32)
        uu = jnp.dot(h, wu_ref[:, sl], preferred_element_type=F32)
        a_scr[:, sl] = ((gg * _sigmoid(gg)) * uu).astype(BF16)
    y = jnp.dot(a_scr[...], wd_ref[...], preferred_element_type=F32)
    gate = mod_ref[0, 3 * idx + 2:3 * idx + 3, :]
    out = x + (0.5 * gate) * y
    if final:
        out = _rms_norm(out, fg_ref[...])
    o_ref[0] = out


def _ffn(x, mod, g, wg, wu, wd, fg, *, idx, final):
    b, s, d = x.shape
    t = FFN_T
    return pl.pallas_call(
        functools.partial(_ffn_kernel, idx=idx, final=final),
        grid=(b, s // t),
        in_specs=[
            pl.BlockSpec((1, t, d), lambda i, j: (i, j, 0)),
            pl.BlockSpec((1, N_MOD, d), lambda i, j: (i, 0, 0)),
            _const_spec((1, d)),
            _const_spec((d, D_FF)),
            _const_spec((d, D_FF)),
            _const_spec((D_FF, d)),
            _const_spec((1, d)),
        ],
        out_specs=pl.BlockSpec((1, t, d), lambda i, j: (i, j, 0)),
        out_shape=jax.ShapeDtypeStruct((b, s, d), F32),
        scratch_shapes=[pltpu.VMEM((t, D_FF), BF16)],
        compiler_params=_params(2),
        name="ffn_final" if final else "ffn",
    )(x, mod, g, wg, wu, wd, fg)


PRE_T = 512
PRE_CH = 256


def _causal_conv(hbuf, cwb_ref, cb_ref, conv_scr, t, between):
    first = HALO - (CONV_K - 1)
    n_a = (first + CONV_K - 1) // SUBLANES + 1
    n = CONV_ROWS // SUBLANES + 1
    zeros = []
    for c in range(D_MODEL // LANES):
        cols = slice(c * LANES, (c + 1) * LANES)
        wv = [cwb_ref[SUBLANES * j:SUBLANES * (j + 1), cols][None] for j in range(CONV_K)]
        bias = cb_ref[:, cols]
        if c >= 2:
            bias = bias + zeros[c - 2]
        for rb in range(t // CONV_ROWS):
            r0 = rb * CONV_ROWS
            hs = [hbuf[r0 + SUBLANES * a:r0 + SUBLANES * a + CONV_ROWS + SUBLANES, cols]
                  .reshape(n, SUBLANES, LANES) for a in range(n_a)]
            acc = None
            for b in range(SUBLANES):
                part = None
                for a in range(n_a):
                    j = SUBLANES * a + b - first
                    if 0 <= j < CONV_K:
                        term = hs[a] * wv[j]
                        part = term if part is None else part + term
                part = part.reshape(CONV_ROWS + SUBLANES, LANES)[b:b + CONV_ROWS]
                acc = part if acc is None else acc + part
            conv_scr[r0:r0 + CONV_ROWS, cols] = acc + bias
        zeros.append(between(c))
    return zeros[-2:]


def _zero_from(v):
    tile = v[0:SUBLANES, 0:LANES]
    bits = tile if tile.dtype == jnp.uint32 else pltpu.bitcast(tile, jnp.uint32)
    return pltpu.bitcast((bits >> 16) >> 16, F32)[0:1, :]


def _pack_pair(a, b):
    a_bits = pltpu.bitcast(a.astype(BF16).astype(F32), jnp.uint32)
    b_bits = pltpu.bitcast(b.astype(BF16).astype(F32), jnp.uint32)
    return a_bits | (b_bits >> 16)


def _unpack_pair(w, which):
    bits = (w & jnp.uint32(0xFFFF0000)) if which == 0 else (w << 16)
    return pltpu.bitcast(bits, F32).astype(BF16)


def _pre_kernel(x_ref, mod_ref, g_ref, w_ref, cos_ref, sin_ref,
                cwb_ref, cb_ref, lng_ref, lnb_ref, cwo_ref, *rest):
    qkv_refs = rest[:3 * N_GROUPS]
    ga_ref, yc_ref, hbuf, conv_scr = rest[3 * N_GROUPS:]
    j = pl.program_id(1)
    t = PRE_T

    @pl.when(j == 0)
    def _():
        hbuf[0:HALO, :] = jnp.zeros((HALO, D_MODEL), F32)
        hbuf[HALO + t:, :] = jnp.zeros((SUBLANES, D_MODEL), F32)

    x = x_ref[0]
    h = _modulate(x, g_ref[...], mod_ref, 1).astype(BF16)
    cos = cos_ref[...]
    sin = sin_ref[...]
    lane = lax.broadcasted_iota(jnp.int32, (t, LANES), 1)
    low_half = (lane % HEAD_DIM) < (HEAD_DIM // 2)

    def proj(c0):
        return jnp.dot(h, w_ref[:, c0:c0 + PRE_CH], preferred_element_type=F32)

    u0 = 3 * QKV_W
    g0 = u0 + 2 * D_MODEL
    for c in range(D_MODEL // PRE_CH):
        za = proj(u0 + c * PRE_CH)
        zg = proj(u0 + D_MODEL + c * PRE_CH)
        hbuf[HALO:HALO + t, c * PRE_CH:(c + 1) * PRE_CH] = za * _sigmoid(zg)

    def rope(z):
        rot = jnp.where(low_half,
                        pltpu.roll(z, LANES - HEAD_DIM // 2, 1),
                        pltpu.roll(z, HEAD_DIM // 2, 1))
        return z * cos + rot * sin

    lo, hi = slice(0, LANES), slice(LANES, 2 * LANES)
    scale = HEAD_DIM ** -0.5

    def q_item(g):
        z = proj(g * GROUP_W)
        w = _pack_pair(rope(z[:, lo]) * scale, rope(z[:, hi]) * scale)
        qkv_refs[g][0] = w
        return w

    def k_item(g):
        z = proj(QKV_W + g * GROUP_W)
        w = _pack_pair(rope(z[:, lo]), rope(z[:, hi]))
        qkv_refs[N_GROUPS + g][0] = w
        return w

    def v_item(g):
        z = proj(2 * QKV_W + g * GROUP_W)
        w = _pack_pair(z[:, lo], z[:, hi])
        qkv_refs[2 * N_GROUPS + g][0] = w
        return w

    def gate_item(ref, c0, c):
        s = _sigmoid(proj(c0 + c * PRE_CH))
        ref[0, :, c * PRE_CH:(c + 1) * PRE_CH] = s
        return s

    items = []
    for g in range(N_GROUPS):
        items += [functools.partial(q_item, g), functools.partial(k_item, g),
                  functools.partial(v_item, g)]
    for c in range(D_MODEL // PRE_CH):
        items += [functools.partial(gate_item, ga_ref, g0, c),
                  functools.partial(gate_item, yc_ref, g0 + D_MODEL, c)]
    n_chunks = D_MODEL // LANES

    def between(c):
        zero = jnp.zeros((1, LANES), F32)
        for item in items[c * len(items) // n_chunks:(c + 1) * len(items) // n_chunks]:
            zero = zero + _zero_from(item())
        return zero

    tail_zeros = _causal_conv(hbuf, cwb_ref, cb_ref, conv_scr, t, between)
    hbuf[0:HALO, :] = hbuf[t:t + HALO, :]

    hv = conv_scr[...]
    mu = jnp.mean(hv, axis=-1, keepdims=True)
    xc = hv - mu
    var = jnp.mean(xc * xc, axis=-1, keepdims=True)
    ln_b = lnb_ref[...] + jnp.tile(tail_zeros[0] + tail_zeros[1], (1, D_MODEL // LANES))
    hn = xc * lax.rsqrt(var + EPS) * lng_ref[...] + ln_b
    hn = hn * _sigmoid(hn)
    y_conv = jnp.dot(hn.astype(BF16), cwo_ref[...], preferred_element_type=F32)
    yc_ref[0] = yc_ref[0] * y_conv


def _pre(x, mod, g, w_in, cos, sin, conv_w, conv_b, ln_g, ln_b, conv_wo):
    b, s, d = x.shape
    t = PRE_T
    tok = lambda w: pl.BlockSpec((1, t, w), lambda i, j: (i, j, 0))
    conv_wb = jnp.repeat(conv_w, SUBLANES, axis=0)
    n_qkv = 3 * N_GROUPS
    outs = pl.pallas_call(
        _pre_kernel,
        grid=(b, s // t),
        in_specs=[
            tok(d),
            pl.BlockSpec((1, N_MOD, d), lambda i, j: (i, 0, 0)),
            _const_spec((1, d)),
            _const_spec((d, N_IN)),
            pl.BlockSpec((t, LANES), lambda i, j: (j, 0)),
            pl.BlockSpec((t, LANES), lambda i, j: (j, 0)),
            _const_spec((CONV_K * SUBLANES, d)),
            _const_spec((1, d)), _const_spec((1, d)), _const_spec((1, d)),
            _const_spec((d, d)),
        ],
        out_specs=[tok(LANES)] * n_qkv + [tok(d), tok(d)],
        out_shape=[jax.ShapeDtypeStruct((b, s, LANES), jnp.uint32)] * n_qkv + [
            jax.ShapeDtypeStruct((b, s, d), F32),
            jax.ShapeDtypeStruct((b, s, d), F32),
        ],
        scratch_shapes=[pltpu.VMEM((HALO + t + SUBLANES, d), F32), pltpu.VMEM((t, d), F32)],
        compiler_params=_params(2),
        name="pre",
    )(x, mod, g, w_in, cos, sin, conv_wb, conv_b, ln_g, ln_b, conv_wo)
    return outs[:n_qkv], outs[n_qkv], outs[n_qkv + 1]


ATTN_T = 2048
N_PAIRS = GROUP_W // (2 * HEAD_DIM)
ATTN_UNROLL = 2


def _rows(start, dil):
    if dil == 1:
        return pl.ds(start, BLK)
    return pl.ds(start, BLK, stride=dil)


def _attn_kernel(*refs):
    g_refs = [refs[5 * g:5 * g + 5] for g in range(N_GROUPS)]
    o_ref, m_scr, a_scr, s_scr = refs[5 * N_GROUPS:]
    n = pl.program_id(1)
    qi = lax.broadcasted_iota(jnp.int32, (BLK, 2 * BLK), 0)
    kj = lax.broadcasted_iota(jnp.int32, (BLK, 2 * BLK), 1)
    band = (kj >= qi) & (kj <= qi + BLK)
    band_first = band & (kj >= jnp.where(n == 0, BLK, 0))
    lane = lax.broadcasted_iota(jnp.int32, (BLK, LANES), 1)
    head0 = lane < HEAD_DIM
    zero = jnp.zeros((BLK, LANES), BF16)

    def heads(qw, kw, vw, mask):
        res = []
        for pair in range(N_PAIRS):
            qp, kp, vp = (_unpack_pair(w, pair) for w in (qw, kw, vw))
            outs, lses = [], []
            for hh in range(2):
                sel = head0 if hh == 0 else jnp.logical_not(head0)
                qh = jnp.where(sel, qp, zero)
                sc = lax.dot_general(qh, kp, (((1,), (1,)), ((), ())),
                                     preferred_element_type=F32)
                sc = jnp.where(mask, sc, NEG_INF)
                mx = jnp.max(sc, axis=-1, keepdims=True)
                p = jnp.exp(sc - mx)
                den = jnp.sum(p, axis=-1, keepdims=True)
                oh = jnp.dot(p.astype(BF16), vp, preferred_element_type=F32)
                outs.append(oh / den)
                lses.append(jnp.broadcast_to(mx + jnp.log(den), (BLK, LANES)))
            res.append((jnp.where(head0, outs[0], outs[1]), jnp.where(head0, lses[0], lses[1])))
        return res

    def merge(g, rows, res):
        for pair, (o, lse) in enumerate(res):
            if g == 0:
                m_scr[pair, rows, :] = lse
                a_scr[pair, rows, :] = o
                continue
            m_old = m_scr[pair, rows, :]
            m_new = jnp.maximum(m_old, lse)
            e_old, e_new = jnp.exp(m_old - m_new), jnp.exp(lse - m_new)
            acc = a_scr[pair, rows, :] * e_old + o * e_new
            if g == 1:
                m_scr[pair, rows, :] = m_new
                a_scr[pair, rows, :] = acc
                s_scr[pair, rows, :] = e_old + e_new
            else:
                o_ref[0, pair, rows, :] = acc / (s_scr[pair, rows, :] * e_old + e_new)

    for g, (q_ref, k_ref, v_ref, kh_ref, vh_ref) in enumerate(g_refs):
        dil = ATTN_GROUPS[g][1]
        win = BLK * dil
        q2, k2, v2, kh2, vh2 = (r.at[0] for r in (q_ref, k_ref, v_ref, kh_ref, vh_ref))

        def block(start, k_prev, v_prev, mask, g=g, dil=dil, q2=q2, k2=k2, v2=v2):
            rows = _rows(start, dil)
            kw = jnp.concatenate([k_prev, k2[rows, :]], axis=0)
            vw = jnp.concatenate([v_prev, v2[rows, :]], axis=0)
            merge(g, rows, heads(q2[rows, :], kw, vw, mask))

        def first(r, carry, dil=dil, kh2=kh2, vh2=vh2, block=block):
            rows = _rows(r, dil)
            block(r, kh2[rows, :], vh2[rows, :], band_first)
            return carry

        def later(idx, carry, dil=dil, win=win, k2=k2, v2=v2, block=block):
            shift = dil.bit_length() - 1
            start = (lax.shift_right_logical(idx, shift) + 1) * win + (idx & (dil - 1))
            if dil == 1:
                start = pl.multiple_of(start, BLK)
            prev = _rows(start - win, dil)
            block(start, k2[prev, :], v2[prev, :], band)
            return carry

        n_later = (ATTN_T // win - 1) * dil
        lax.fori_loop(0, dil, first, 0, unroll=min(dil, ATTN_UNROLL))
        if n_later:
            lax.fori_loop(0, n_later, later, 0,
                          unroll=ATTN_UNROLL if n_later % ATTN_UNROLL == 0 else ATTN_UNROLL + 1)


def _attn(qkv):
    b, s, _ = qkv[0].shape
    t = ATTN_T
    cur = pl.BlockSpec((1, t, LANES), lambda i, n: (i, n, 0))
    args, specs = [], []
    for g, (_, dil) in enumerate(ATTN_GROUPS):
        win = BLK * dil
        per = t // win
        halo = pl.BlockSpec((1, win, LANES),
                            lambda i, n, per=per: (i, jnp.maximum(n * per - 1, 0), 0))
        q, k, v = qkv[g], qkv[N_GROUPS + g], qkv[2 * N_GROUPS + g]
        args += [q, k, v, k, v]
        specs += [cur, cur, cur, halo, halo]
    state = pltpu.VMEM((N_PAIRS, t, LANES), F32)
    return pl.pallas_call(
        _attn_kernel,
        grid=(b, s // t),
        in_specs=specs,
        out_specs=pl.BlockSpec((1, N_PAIRS, t, LANES), lambda i, n: (i, 0, n, 0)),
        out_shape=jax.ShapeDtypeStruct((b, N_PAIRS, s, LANES), F32),
        scratch_shapes=[state, state, state],
        compiler_params=_params(2),
        name="attn",
    )(*args)


POST_T = 512


def _post_kernel(x_ref, mod_ref, ga_ref, yc_ref, o_ref, awo_ref, wout_ref, out_ref):
    o = jnp.concatenate([o_ref[0, p] for p in range(N_PAIRS)], axis=-1)
    y_attn = jnp.dot(o.astype(BF16), awo_ref[...], preferred_element_type=F32)
    y = ga_ref[0] * y_attn + yc_ref[0]
    gate = mod_ref[0, 5:6, :]
    out_ref[0] = x_ref[0] + gate * jnp.dot(y.astype(BF16), wout_ref[...],
                                           preferred_element_type=F32)


def _post(x, mod, ga, yc, o, attn_wo, w_out):
    b, s, d = x.shape
    t = POST_T
    tok = lambda w: pl.BlockSpec((1, t, w), lambda i, j: (i, j, 0))
    return pl.pallas_call(
        _post_kernel,
        grid=(b, s // t),
        in_specs=[tok(d), pl.BlockSpec((1, N_MOD, d), lambda i, j: (i, 0, 0)),
                  tok(d), tok(d),
                  pl.BlockSpec((1, N_PAIRS, t, LANES), lambda i, j: (i, 0, j, 0)),
                  _const_spec((GROUP_W, d)), _const_spec((d, d))],
        out_specs=tok(d),
        out_shape=jax.ShapeDtypeStruct((b, s, d), F32),
        compiler_params=_params(2),
        name="post",
    )(x, mod, ga, yc, o, attn_wo, w_out)


def _rope_tables(s):
    half = HEAD_DIM // 2
    inv_freq = ROPE_THETA ** (-(jnp.arange(half, dtype=F32) * 2.0 / HEAD_DIM))
    ang = jnp.arange(s, dtype=F32)[:, None] * inv_freq[None, :]
    cos, sin = jnp.cos(ang), jnp.sin(ang)
    reps = LANES // HEAD_DIM
    cos_t = jnp.tile(jnp.concatenate([cos, cos], axis=-1), (1, reps))
    sin_t = jnp.tile(jnp.concatenate([-sin, sin], axis=-1), (1, reps))
    return cos_t, sin_t


def kernel(x, c, ada_w, ada_b, norm_g, ffn_wg, ffn_wu, ffn_wd, w_in, attn_wo,
           conv_w, conv_b, conv_ln_g, conv_ln_b, conv_wo, w_out, final_g):
    b, s, d = x.shape
    cos_t, sin_t = _rope_tables(s)
    mod_all = _ada(c, ada_w, ada_b)
    row = lambda v: v.reshape(1, d)
    fg = row(final_g)
    for l in range(DEPTH):
        mod = mod_all[l]
        wg, wu, wd = (w[l].astype(BF16) for w in (ffn_wg, ffn_wu, ffn_wd))
        x = _ffn(x, mod, row(norm_g[l, 0]), wg[0], wu[0], wd[0], fg, idx=0, final=False)
        qkv, ga, yc = _pre(x, mod, row(norm_g[l, 1]), w_in[l].astype(BF16), cos_t, sin_t,
                           conv_w[l], row(conv_b[l]), row(conv_ln_g[l]),
                           row(conv_ln_b[l]), conv_wo[l].astype(BF16))
        x = _post(x, mod, ga, yc, _attn(qkv), attn_wo[l].astype(BF16), w_out[l].astype(BF16))
        x = _ffn(x, mod, row(norm_g[l, 2]), wg[1], wu[1], wd[1], fg, idx=2,
                 final=(l == DEPTH - 1))
    return x
```

```python
import functools

import jax
import jax.numpy as jnp
from jax import lax
from jax.experimental import pallas as pl
from jax.experimental.pallas import tpu as pltpu

D_MODEL = 1024
DEPTH = 2
HEAD_DIM = 64
HEADS_PER_GROUP = 4
ATTN_GROUPS = ((128, 1), (512, 4), (2048, 16))
N_GROUPS = len(ATTN_GROUPS)
GROUP_W = HEADS_PER_GROUP * HEAD_DIM
QKV_W = N_GROUPS * GROUP_W
CONV_K = 31
D_FF = 2816
ROPE_THETA = 10000.0
EPS = 1e-6
N_MOD = 9
NEG_INF = -1e30
N_IN = 3 * QKV_W + 2 * D_MODEL + 2 * D_MODEL

LANES = 128
SUBLANES = 8
BLK = 128
HALO = 32
CONV_ROWS = 64
VMEM_LIMIT = 56 * 1024 * 1024

F32 = jnp.float32
BF16 = jnp.bfloat16


def _const_spec(shape, lead=()):
    index = tuple(lead) + (0,) * len(shape)
    return pl.BlockSpec((None,) * len(lead) + tuple(shape), lambda *_: index,
                        pipeline_mode=pl.Buffered(1))


def _params(n_axes):
    return pltpu.CompilerParams(
        dimension_semantics=("arbitrary",) * n_axes, vmem_limit_bytes=VMEM_LIMIT)


def _sigmoid(v):
    return 1.0 / (1.0 + jnp.exp(-v))


def _rms_norm(x, g):
    return (x * lax.rsqrt(jnp.mean(x * x, axis=-1, keepdims=True) + EPS)) * g


def _modulate(x, g, mod_ref, i):
    shift = mod_ref[0, 3 * i:3 * i + 1, :]
    scale = mod_ref[0, 3 * i + 1:3 * i + 2, :]
    return _rms_norm(x, g) * (1.0 + scale) + shift


def _ada_kernel(c_ref, w_ref, b_ref, o_ref):
    c = c_ref[...]
    ca = c * _sigmoid(c)
    o_ref[0] = jnp.dot(ca, w_ref[0], preferred_element_type=F32,
                       precision=lax.Precision.HIGHEST) + b_ref[0]


def _ada(c, ada_w, ada_b):
    b = c.shape[0]
    rows = 8
    c_pad = jnp.pad(c, ((0, rows - b), (0, 0)))
    n = N_MOD * D_MODEL
    tn = D_MODEL
    out = pl.pallas_call(
        _ada_kernel,
        grid=(DEPTH, n // tn),
        in_specs=[
            pl.BlockSpec((rows, D_MODEL), lambda l, j: (0, 0)),
            pl.BlockSpec((1, D_MODEL, tn), lambda l, j: (l, 0, j)),
            pl.BlockSpec((1, 1, tn), lambda l, j: (l, 0, j)),
        ],
        out_specs=pl.BlockSpec((1, rows, tn), lambda l, j: (l, 0, j)),
        out_shape=jax.ShapeDtypeStruct((DEPTH, rows, n), F32),
        compiler_params=_params(2),
        name="ada",
    )(c_pad, ada_w, ada_b.reshape(DEPTH, 1, n))
    return out[:, :b].reshape(DEPTH, b, N_MOD, D_MODEL)


FFN_T = 512
FFN_CH = 256


def _ffn_kernel(*refs, idx, mix, final):
    if mix:
        ga_ref, yc_ref, oa_ref, awo_ref, wout_ref = refs[:5]
        refs = refs[5:]
    x_ref, mod_ref, g_ref, wg_ref, wu_ref, wd_ref, fg_ref, o_ref, a_scr = refs
    x = x_ref[0]
    if mix:
        o = jnp.concatenate([oa_ref[0, p] for p in range(N_PAIRS)], axis=-1)
        y_attn = jnp.dot(o.astype(BF16), awo_ref[...], preferred_element_type=F32)
        y = ga_ref[0] * y_attn + yc_ref[0]
        x = x + mod_ref[0, 5:6, :] * jnp.dot(y.astype(BF16), wout_ref[...],
                                             preferred_element_type=F32)
    h = _modulate(x, g_ref[...], mod_ref, idx).astype(BF16)
    for c in range(D_FF // FFN_CH):
        sl = slice(c * FFN_CH, (c + 1) * FFN_CH)
        gg = jnp.dot(h, wg_ref[:, sl], preferred_element_type=F32)
        uu = jnp.dot(h, wu_ref[:, sl], preferred_element_type=F32)
        a_scr[:, sl] = ((gg * _sigmoid(gg)) * uu).astype(BF16)
    y = jnp.dot(a_scr[...], wd_ref[...], preferred_element_type=F32)
    gate = mod_ref[0, 3 * idx + 2:3 * idx + 3, :]
    out = x + (0.5 * gate) * y
    if final:
        out = _rms_norm(out, fg_ref[...])
    o_ref[0] = out


def _ffn(x, mod, norm_g, wg, wu, wd, fg, l, half, *, mix=None, final=False):
    b, s, d = x.shape
    t = FFN_T
    idx = 2 * half
    tok = pl.BlockSpec((1, t, d), lambda i, j: (i, j, 0))
    args, specs = [], []
    if mix is not None:
        args += list(mix)
        specs += [tok, tok, pl.BlockSpec((1, N_PAIRS, t, LANES), lambda i, j: (i, 0, j, 0)),
                  _const_spec((GROUP_W, d), (l,)), _const_spec((d, d), (l,))]
    args += [x, mod, norm_g, wg, wu, wd, fg]
    specs += [tok, pl.BlockSpec((1, N_MOD, d), lambda i, j: (i, 0, 0)),
              _const_spec((1, d), (l * 3 + idx,)),
              _const_spec((d, D_FF), (l, half)), _const_spec((d, D_FF), (l, half)),
              _const_spec((D_FF, d), (l, half)), _const_spec((1, d))]
    return pl.pallas_call(
        functools.partial(_ffn_kernel, idx=idx, mix=mix is not None, final=final),
        grid=(b, s // t),
        in_specs=specs,
        out_specs=tok,
        out_shape=jax.ShapeDtypeStruct((b, s, d), F32),
        scratch_shapes=[pltpu.VMEM((t, D_FF), BF16)],
        compiler_params=_params(2),
        name="ffn_mix" if mix is not None else "ffn",
    )(*args)


PRE_T = 512
PRE_CH = 256


def _causal_conv(hbuf, cwb_ref, cb_ref, conv_scr, t, between):
    first = HALO - (CONV_K - 1)
    n_a = (first + CONV_K - 1) // SUBLANES + 1
    n = CONV_ROWS // SUBLANES + 1
    zeros = []
    for c in range(D_MODEL // LANES):
        cols = slice(c * LANES, (c + 1) * LANES)
        wv = [cwb_ref[SUBLANES * j:SUBLANES * (j + 1), cols][None] for j in range(CONV_K)]
        bias = cb_ref[:, cols]
        if c >= 2:
            bias = bias + zeros[c - 2]
        for rb in range(t // CONV_ROWS):
            r0 = rb * CONV_ROWS
            hs = [hbuf[r0 + SUBLANES * a:r0 + SUBLANES * a + CONV_ROWS + SUBLANES, cols]
                  .reshape(n, SUBLANES, LANES) for a in range(n_a)]
            acc = None
            for b in range(SUBLANES):
                part = None
                for a in range(n_a):
                    j = SUBLANES * a + b - first
                    if 0 <= j < CONV_K:
                        term = hs[a] * wv[j]
                        part = term if part is None else part + term
                part = part.reshape(CONV_ROWS + SUBLANES, LANES)[b:b + CONV_ROWS]
                acc = part if acc is None else acc + part
            conv_scr[r0:r0 + CONV_ROWS, cols] = acc + bias
        zeros.append(between(c))
    return zeros[-2:]


def _zero_from(v):
    tile = v[0:SUBLANES, 0:LANES]
    bits = tile if tile.dtype == jnp.uint32 else pltpu.bitcast(tile, jnp.uint32)
    return pltpu.bitcast((bits >> 16) >> 16, F32)[0:1, :]


def _pack_pair(a, b):
    a_bits = pltpu.bitcast(a.astype(BF16).astype(F32), jnp.uint32)
    b_bits = pltpu.bitcast(b.astype(BF16).astype(F32), jnp.uint32)
    return a_bits | (b_bits >> 16)


def _unpack_pair(w, which):
    bits = (w & jnp.uint32(0xFFFF0000)) if which == 0 else (w << 16)
    return pltpu.bitcast(bits, F32).astype(BF16)


def _pre_kernel(x_ref, mod_ref, g_ref, w_ref, cos_ref, sin_ref,
                cwb_ref, cb_ref, lng_ref, lnb_ref, cwo_ref, *rest):
    qkv_refs = rest[:3 * N_GROUPS]
    ga_ref, yc_ref, hbuf, conv_scr = rest[3 * N_GROUPS:]
    j = pl.program_id(1)
    t = PRE_T

    @pl.when(j == 0)
    def _():
        hbuf[0:HALO, :] = jnp.zeros((HALO, D_MODEL), F32)
        hbuf[HALO + t:, :] = jnp.zeros((SUBLANES, D_MODEL), F32)

    x = x_ref[0]
    h = _modulate(x, g_ref[...], mod_ref, 1).astype(BF16)
    cos = cos_ref[...]
    sin = sin_ref[...]
    lane = lax.broadcasted_iota(jnp.int32, (t, LANES), 1)
    low_half = (lane % HEAD_DIM) < (HEAD_DIM // 2)

    def proj(c0):
        return jnp.dot(h, w_ref[:, c0:c0 + PRE_CH], preferred_element_type=F32)

    u0 = 3 * QKV_W
    g0 = u0 + 2 * D_MODEL
    for c in range(D_MODEL // PRE_CH):
        za = proj(u0 + c * PRE_CH)
        zg = proj(u0 + D_MODEL + c * PRE_CH)
        hbuf[HALO:HALO + t, c * PRE_CH:(c + 1) * PRE_CH] = za * _sigmoid(zg)

    def rope(z):
        rot = jnp.where(low_half,
                        pltpu.roll(z, LANES - HEAD_DIM // 2, 1),
                        pltpu.roll(z, HEAD_DIM // 2, 1))
        return z * cos + rot * sin

    lo, hi = slice(0, LANES), slice(LANES, 2 * LANES)
    scale = HEAD_DIM ** -0.5

    def q_item(g):
        z = proj(g * GROUP_W)
        w = _pack_pair(rope(z[:, lo]) * scale, rope(z[:, hi]) * scale)
        qkv_refs[g][0] = w
        return w

    def k_item(g):
        z = proj(QKV_W + g * GROUP_W)
        w = _pack_pair(rope(z[:, lo]), rope(z[:, hi]))
        qkv_refs[N_GROUPS + g][0] = w
        return w

    def v_item(g):
        z = proj(2 * QKV_W + g * GROUP_W)
        w = _pack_pair(z[:, lo], z[:, hi])
        qkv_refs[2 * N_GROUPS + g][0] = w
        return w

    def gate_item(ref, c0, c):
        s = _sigmoid(proj(c0 + c * PRE_CH))
        ref[0, :, c * PRE_CH:(c + 1) * PRE_CH] = s
        return s

    items = []
    for g in range(N_GROUPS):
        items += [functools.partial(q_item, g), functools.partial(k_item, g),
                  functools.partial(v_item, g)]
    for c in range(D_MODEL // PRE_CH):
        items += [functools.partial(gate_item, ga_ref, g0, c),
                  functools.partial(gate_item, yc_ref, g0 + D_MODEL, c)]
    n_chunks = D_MODEL // LANES

    def between(c):
        zero = jnp.zeros((1, LANES), F32)
        for item in items[c * len(items) // n_chunks:(c + 1) * len(items) // n_chunks]:
            zero = zero + _zero_from(item())
        return zero

    tail_zeros = _causal_conv(hbuf, cwb_ref, cb_ref, conv_scr, t, between)
    hbuf[0:HALO, :] = hbuf[t:t + HALO, :]

    hv = conv_scr[...]
    mu = jnp.mean(hv, axis=-1, keepdims=True)
    xc = hv - mu
    var = jnp.mean(xc * xc, axis=-1, keepdims=True)
    ln_b = lnb_ref[...] + jnp.tile(tail_zeros[0] + tail_zeros[1], (1, D_MODEL // LANES))
    hn = xc * lax.rsqrt(var + EPS) * lng_ref[...] + ln_b
    hn = hn * _sigmoid(hn)
    y_conv = jnp.dot(hn.astype(BF16), cwo_ref[...], preferred_element_type=F32)
    yc_ref[0] = yc_ref[0] * y_conv


def _pre(x, mod, norm_g, w_in, cos, sin, conv_wb, conv_b, ln_g, ln_b, conv_wo, l):
    b, s, d = x.shape
    t = PRE_T
    tok = lambda w: pl.BlockSpec((1, t, w), lambda i, j: (i, j, 0))
    n_qkv = 3 * N_GROUPS
    outs = pl.pallas_call(
        _pre_kernel,
        grid=(b, s // t),
        in_specs=[
            tok(d),
            pl.BlockSpec((1, N_MOD, d), lambda i, j: (i, 0, 0)),
            _const_spec((1, d), (l * 3 + 1,)),
            _const_spec((d, N_IN), (l,)),
            pl.BlockSpec((t, LANES), lambda i, j: (j, 0)),
            pl.BlockSpec((t, LANES), lambda i, j: (j, 0)),
            _const_spec((CONV_K * SUBLANES, d), (l,)),
            _const_spec((1, d), (l,)), _const_spec((1, d), (l,)), _const_spec((1, d), (l,)),
            _const_spec((d, d), (l,)),
        ],
        out_specs=[tok(LANES)] * n_qkv + [tok(d), tok(d)],
        out_shape=[jax.ShapeDtypeStruct((b, s, LANES), jnp.uint32)] * n_qkv + [
            jax.ShapeDtypeStruct((b, s, d), F32),
            jax.ShapeDtypeStruct((b, s, d), F32),
        ],
        scratch_shapes=[pltpu.VMEM((HALO + t + SUBLANES, d), F32), pltpu.VMEM((t, d), F32)],
        compiler_params=_params(2),
        name="pre",
    )(x, mod, norm_g, w_in, cos, sin, conv_wb, conv_b, ln_g, ln_b, conv_wo)
    return outs[:n_qkv], outs[n_qkv], outs[n_qkv + 1]


ATTN_T = 2048
N_PAIRS = GROUP_W // (2 * HEAD_DIM)
ATTN_UNROLL = 4


def _rows(start, dil):
    if dil == 1:
        return pl.ds(start, BLK)
    return pl.ds(start, BLK, stride=dil)


def _attn_kernel(*refs):
    g_refs = [refs[5 * g:5 * g + 5] for g in range(N_GROUPS)]
    o_ref, m_scr, a_scr, s_scr = refs[5 * N_GROUPS:]
    n = pl.program_id(1)
    qi = lax.broadcasted_iota(jnp.int32, (BLK, 2 * BLK), 0)
    kj = lax.broadcasted_iota(jnp.int32, (BLK, 2 * BLK), 1)
    band = (kj >= qi) & (kj <= qi + BLK)
    band_first = band & (kj >= jnp.where(n == 0, BLK, 0))
    lane = lax.broadcasted_iota(jnp.int32, (BLK, LANES), 1)
    head0 = lane < HEAD_DIM
    zero = jnp.zeros((BLK, LANES), BF16)

    def heads(qw, kw, vw, mask):
        res = []
        for pair in range(N_PAIRS):
            qp, kp, vp = (_unpack_pair(w, pair) for w in (qw, kw, vw))
            outs, lses = [], []
            for hh in range(2):
                sel = head0 if hh == 0 else jnp.logical_not(head0)
                qh = jnp.where(sel, qp, zero)
                sc = lax.dot_general(qh, kp, (((1,), (1,)), ((), ())),
                                     preferred_element_type=F32)
                sc = jnp.where(mask, sc, NEG_INF)
                mx = jnp.max(sc, axis=-1, keepdims=True)
                p = jnp.exp(sc - mx)
                den = jnp.sum(p, axis=-1, keepdims=True)
                oh = jnp.dot(p.astype(BF16), vp, preferred_element_type=F32)
                outs.append(oh / den)
                lses.append(jnp.broadcast_to(mx + jnp.log(den), (BLK, LANES)))
            res.append((jnp.where(head0, outs[0], outs[1]), jnp.where(head0, lses[0], lses[1])))
        return res

    def merge(g, rows, res):
        for pair, (o, lse) in enumerate(res):
            if g == 0:
                m_scr[pair, rows, :] = lse
                a_scr[pair, rows, :] = o
                continue
            m_old = m_scr[pair, rows, :]
            m_new = jnp.maximum(m_old, lse)
            e_old, e_new = jnp.exp(m_old - m_new), jnp.exp(lse - m_new)
            acc = a_scr[pair, rows, :] * e_old + o * e_new
            if g == 1:
                m_scr[pair, rows, :] = m_new
                a_scr[pair, rows, :] = acc
                s_scr[pair, rows, :] = e_old + e_new
            else:
                o_ref[0, pair, rows, :] = acc / (s_scr[pair, rows, :] * e_old + e_new)

    for g, (q_ref, k_ref, v_ref, kh_ref, vh_ref) in enumerate(g_refs):
        dil = ATTN_GROUPS[g][1]
        win = BLK * dil
        q2, k2, v2, kh2, vh2 = (r.at[0] for r in (q_ref, k_ref, v_ref, kh_ref, vh_ref))

        def block(start, k_prev, v_prev, mask, g=g, dil=dil, q2=q2, k2=k2, v2=v2):
            rows = _rows(start, dil)
            kw = jnp.concatenate([k_prev, k2[rows, :]], axis=0)
            vw = jnp.concatenate([v_prev, v2[rows, :]], axis=0)
            merge(g, rows, heads(q2[rows, :], kw, vw, mask))

        def first(r, carry, dil=dil, kh2=kh2, vh2=vh2, block=block):
            rows = _rows(r, dil)
            block(r, kh2[rows, :], vh2[rows, :], band_first)
            return carry

        def later(idx, carry, dil=dil, win=win, k2=k2, v2=v2, block=block):
            shift = dil.bit_length() - 1
            start = (lax.shift_right_logical(idx, shift) + 1) * win + (idx & (dil - 1))
            if dil == 1:
                start = pl.multiple_of(start, BLK)
            prev = _rows(start - win, dil)
            block(start, k2[prev, :], v2[prev, :], band)
            return carry

        n_later = (ATTN_T // win - 1) * dil
        lax.fori_loop(0, dil, first, 0, unroll=min(dil, ATTN_UNROLL))
        if n_later:
            lax.fori_loop(0, n_later, later, 0,
                          unroll=ATTN_UNROLL if n_later % ATTN_UNROLL == 0 else ATTN_UNROLL + 1)


def _attn(qkv):
    b, s, _ = qkv[0].shape
    t = ATTN_T
    cur = pl.BlockSpec((1, t, LANES), lambda i, n: (i, n, 0))
    args, specs = [], []
    for g, (_, dil) in enumerate(ATTN_GROUPS):
        win = BLK * dil
        per = t // win
        halo = pl.BlockSpec((1, win, LANES),
                            lambda i, n, per=per: (i, jnp.maximum(n * per - 1, 0), 0))
        q, k, v = qkv[g], qkv[N_GROUPS + g], qkv[2 * N_GROUPS + g]
        args += [q, k, v, k, v]
        specs += [cur, cur, cur, halo, halo]
    state = pltpu.VMEM((N_PAIRS, t, LANES), F32)
    return pl.pallas_call(
        _attn_kernel,
        grid=(b, s // t),
        in_specs=specs,
        out_specs=pl.BlockSpec((1, N_PAIRS, t, LANES), lambda i, n: (i, 0, n, 0)),
        out_shape=jax.ShapeDtypeStruct((b, N_PAIRS, s, LANES), F32),
        scratch_shapes=[state, state, state],
        compiler_params=_params(2),
        name="attn",
    )(*args)


def _rope_tables(s):
    half = HEAD_DIM // 2
    inv_freq = ROPE_THETA ** (-(jnp.arange(half, dtype=F32) * 2.0 / HEAD_DIM))
    ang = jnp.arange(s, dtype=F32)[:, None] * inv_freq[None, :]
    cos, sin = jnp.cos(ang), jnp.sin(ang)
    reps = LANES // HEAD_DIM
    cos_t = jnp.tile(jnp.concatenate([cos, cos], axis=-1), (1, reps))
    sin_t = jnp.tile(jnp.concatenate([-sin, sin], axis=-1), (1, reps))
    return cos_t, sin_t


def kernel(x, c, ada_w, ada_b, norm_g, ffn_wg, ffn_wu, ffn_wd, w_in, attn_wo,
           conv_w, conv_b, conv_ln_g, conv_ln_b, conv_wo, w_out, final_g):
    b, s, d = x.shape
    cos_t, sin_t = _rope_tables(s)
    mod_all = _ada(c, ada_w, ada_b)
    rows = lambda v: v.reshape(-1, 1, d)
    norm_g, conv_b, conv_ln_g, conv_ln_b = (rows(v) for v in (norm_g, conv_b, conv_ln_g, conv_ln_b))
    fg = final_g.reshape(1, d)
    wg, wu, wd, w_in, conv_wo, attn_wo, w_out = (
        w.astype(BF16) for w in (ffn_wg, ffn_wu, ffn_wd, w_in, conv_wo, attn_wo, w_out))
    conv_wb = jnp.repeat(conv_w, SUBLANES, axis=1)
    for l in range(DEPTH):
        mod = mod_all[l]
        x = _ffn(x, mod, norm_g, wg, wu, wd, fg, l, 0)
        qkv, ga, yc = _pre(x, mod, norm_g, w_in, cos_t, sin_t, conv_wb, conv_b, conv_ln_g,
                           conv_ln_b, conv_wo, l)
        x = _ffn(x, mod, norm_g, wg, wu, wd, fg, l, 1,
                 mix=(ga, yc, _attn(qkv), attn_wo, w_out), final=(l == DEPTH - 1))
    return x
```

```python
import functools

import jax
import jax.numpy as jnp
from jax import lax
from jax.experimental import pallas as pl
from jax.experimental.pallas import tpu as pltpu

D_MODEL = 1024
DEPTH = 2
HEAD_DIM = 64
HEADS_PER_GROUP = 4
ATTN_GROUPS = ((128, 1), (512, 4), (2048, 16))
N_GROUPS = len(ATTN_GROUPS)
GROUP_W = HEADS_PER_GROUP * HEAD_DIM
QKV_W = N_GROUPS * GROUP_W
CONV_K = 31
D_FF = 2816
ROPE_THETA = 10000.0
EPS = 1e-6
N_MOD = 9
NEG_INF = -1e30
N_IN = 3 * QKV_W + 2 * D_MODEL + 2 * D_MODEL

LANES = 128
SUBLANES = 8
BLK = 128
HALO = 32
CONV_ROWS = 128
VMEM_LIMIT = 56 * 1024 * 1024

F32 = jnp.float32
BF16 = jnp.bfloat16


def _const_spec(shape, lead=()):
    index = tuple(lead) + (0,) * len(shape)
    return pl.BlockSpec((None,) * len(lead) + tuple(shape), lambda *_: index,
                        pipeline_mode=pl.Buffered(1))


def _params(n_axes):
    return pltpu.CompilerParams(
        dimension_semantics=("arbitrary",) * n_axes, vmem_limit_bytes=VMEM_LIMIT)


def _sigmoid(v):
    return 1.0 / (1.0 + jnp.exp(-v))


def _rms_norm(x, g):
    return (x * lax.rsqrt(jnp.mean(x * x, axis=-1, keepdims=True) + EPS)) * g


def _modulate(x, g, mod_ref, i):
    shift = mod_ref[0, 3 * i:3 * i + 1, :]
    scale = mod_ref[0, 3 * i + 1:3 * i + 2, :]
    return _rms_norm(x, g) * (1.0 + scale) + shift


def _ada_kernel(c_ref, w_ref, b_ref, o_ref):
    c = c_ref[...]
    ca = c * _sigmoid(c)
    o_ref[0] = jnp.dot(ca, w_ref[0], preferred_element_type=F32,
                       precision=lax.Precision.HIGHEST) + b_ref[0]


def _ada(c, ada_w, ada_b):
    b = c.shape[0]
    rows = 8
    c_pad = jnp.pad(c, ((0, rows - b), (0, 0)))
    n = N_MOD * D_MODEL
    tn = D_MODEL
    out = pl.pallas_call(
        _ada_kernel,
        grid=(DEPTH, n // tn),
        in_specs=[
            pl.BlockSpec((rows, D_MODEL), lambda l, j: (0, 0)),
            pl.BlockSpec((1, D_MODEL, tn), lambda l, j: (l, 0, j)),
            pl.BlockSpec((1, 1, tn), lambda l, j: (l, 0, j)),
        ],
        out_specs=pl.BlockSpec((1, rows, tn), lambda l, j: (l, 0, j)),
        out_shape=jax.ShapeDtypeStruct((DEPTH, rows, n), F32),
        compiler_params=_params(2),
        name="ada",
    )(c_pad, ada_w, ada_b.reshape(DEPTH, 1, n))
    return out[:, :b].reshape(DEPTH, b, N_MOD, D_MODEL)


FFN_T = 512
FFN_CH = 256


def _ffn_kernel(*refs, idx, mix, final):
    if mix:
        ga_ref, yc_ref, oa_ref, awo_ref, wout_ref = refs[:5]
        refs = refs[5:]
    x_ref, mod_ref, g_ref, wg_ref, wu_ref, wd_ref, fg_ref, o_ref, a_scr = refs
    x = x_ref[0]
    if mix:
        o = jnp.concatenate([oa_ref[0, p] for p in range(N_PAIRS)], axis=-1)
        y_attn = jnp.dot(o.astype(BF16), awo_ref[...], preferred_element_type=F32)
        y = ga_ref[0] * y_attn + yc_ref[0]
        x = x + mod_ref[0, 5:6, :] * jnp.dot(y.astype(BF16), wout_ref[...],
                                             preferred_element_type=F32)
    h = _modulate(x, g_ref[...], mod_ref, idx).astype(BF16)
    for c in range(D_FF // FFN_CH):
        sl = slice(c * FFN_CH, (c + 1) * FFN_CH)
        gg = jnp.dot(h, wg_ref[:, sl], preferred_element_type=F32)
        uu = jnp.dot(h, wu_ref[:, sl], preferred_element_type=F32)
        a_scr[:, sl] = ((gg * _sigmoid(gg)) * uu).astype(BF16)
    y = jnp.dot(a_scr[...], wd_ref[...], preferred_element_type=F32)
    gate = mod_ref[0, 3 * idx + 2:3 * idx + 3, :]
    out = x + (0.5 * gate) * y
    if final:
        out = _rms_norm(out, fg_ref[...])
    o_ref[0] = out


def _ffn(x, mod, norm_g, wg, wu, wd, fg, l, half, *, mix=None, final=False):
    b, s, d = x.shape
    t = FFN_T
    idx = 2 * half
    tok = pl.BlockSpec((1, t, d), lambda i, j: (i, j, 0))
    args, specs = [], []
    if mix is not None:
        args += list(mix)
        specs += [tok, tok, pl.BlockSpec((1, N_PAIRS, t, LANES), lambda i, j: (i, 0, j, 0)),
                  _const_spec((GROUP_W, d), (l,)), _const_spec((d, d), (l,))]
    args += [x, mod, norm_g, wg, wu, wd, fg]
    specs += [tok, pl.BlockSpec((1, N_MOD, d), lambda i, j: (i, 0, 0)),
              _const_spec((1, d), (l * 3 + idx,)),
              _const_spec((d, D_FF), (l, half)), _const_spec((d, D_FF), (l, half)),
              _const_spec((D_FF, d), (l, half)), _const_spec((1, d))]
    return pl.pallas_call(
        functools.partial(_ffn_kernel, idx=idx, mix=mix is not None, final=final),
        grid=(b, s // t),
        in_specs=specs,
        out_specs=tok,
        out_shape=jax.ShapeDtypeStruct((b, s, d), F32),
        scratch_shapes=[pltpu.VMEM((t, D_FF), BF16)],
        compiler_params=_params(2),
        name="ffn_mix" if mix is not None else "ffn",
    )(*args)


PRE_T = 512
PRE_CH = 256


def _causal_conv(hbuf, cwb_ref, cb_ref, conv_scr, t, between):
    first = HALO - (CONV_K - 1)
    n_a = (first + CONV_K - 1) // SUBLANES + 1
    n = CONV_ROWS // SUBLANES + 1
    zeros = []
    for c in range(D_MODEL // LANES):
        cols = slice(c * LANES, (c + 1) * LANES)
        wv = [cwb_ref[SUBLANES * j:SUBLANES * (j + 1), cols][None] for j in range(CONV_K)]
        bias = cb_ref[:, cols]
        if c >= 2:
            bias = bias + zeros[c - 2]
        for rb in range(t // CONV_ROWS):
            r0 = rb * CONV_ROWS
            hs = [hbuf[r0 + SUBLANES * a:r0 + SUBLANES * a + CONV_ROWS + SUBLANES, cols]
                  .reshape(n, SUBLANES, LANES) for a in range(n_a)]
            acc = None
            for b in range(SUBLANES):
                part = None
                for a in range(n_a):
                    j = SUBLANES * a + b - first
                    if 0 <= j < CONV_K:
                        term = hs[a] * wv[j]
                        part = term if part is None else part + term
                part = part.reshape(CONV_ROWS + SUBLANES, LANES)[b:b + CONV_ROWS]
                acc = part if acc is None else acc + part
            conv_scr[r0:r0 + CONV_ROWS, cols] = acc + bias
        zeros.append(between(c))
    return zeros[-2:]


def _zero_from(v):
    tile = v[0:SUBLANES, 0:LANES]
    bits = tile if tile.dtype == jnp.uint32 else pltpu.bitcast(tile, jnp.uint32)
    return pltpu.bitcast((bits >> 16) >> 16, F32)[0:1, :]


def _pack_pair(a, b):
    a_bits = pltpu.bitcast(a.astype(BF16).astype(F32), jnp.uint32)
    b_bits = pltpu.bitcast(b.astype(BF16).astype(F32), jnp.uint32)
    return a_bits | (b_bits >> 16)


def _unpack_pair(w, which):
    bits = (w & jnp.uint32(0xFFFF0000)) if which == 0 else (w << 16)
    return pltpu.bitcast(bits, F32).astype(BF16)


def _pre_kernel(x_ref, mod_ref, g_ref, w_ref, cos_ref, sin_ref,
                cwb_ref, cb_ref, lng_ref, lnb_ref, cwo_ref, *rest):
    qkv_refs = rest[:3 * N_GROUPS]
    ga_ref, yc_ref, hbuf, conv_scr = rest[3 * N_GROUPS:]
    j = pl.program_id(1)
    t = PRE_T

    @pl.when(j == 0)
    def _():
        hbuf[0:HALO, :] = jnp.zeros((HALO, D_MODEL), F32)
        hbuf[HALO + t:, :] = jnp.zeros((SUBLANES, D_MODEL), F32)

    x = x_ref[0]
    h = _modulate(x, g_ref[...], mod_ref, 1).astype(BF16)
    cos = cos_ref[...]
    sin = sin_ref[...]
    lane = lax.broadcasted_iota(jnp.int32, (t, LANES), 1)
    low_half = (lane % HEAD_DIM) < (HEAD_DIM // 2)

    def proj(c0):
        return jnp.dot(h, w_ref[:, c0:c0 + PRE_CH], preferred_element_type=F32)

    u0 = 3 * QKV_W
    g0 = u0 + 2 * D_MODEL
    for c in range(D_MODEL // PRE_CH):
        za = proj(u0 + c * PRE_CH)
        zg = proj(u0 + D_MODEL + c * PRE_CH)
        hbuf[HALO:HALO + t, c * PRE_CH:(c + 1) * PRE_CH] = za * _sigmoid(zg)

    def rope(z):
        rot = jnp.where(low_half,
                        pltpu.roll(z, LANES - HEAD_DIM // 2, 1),
                        pltpu.roll(z, HEAD_DIM // 2, 1))
        return z * cos + rot * sin

    lo, hi = slice(0, LANES), slice(LANES, 2 * LANES)
    scale = HEAD_DIM ** -0.5

    def q_item(g):
        z = proj(g * GROUP_W)
        w = _pack_pair(rope(z[:, lo]) * scale, rope(z[:, hi]) * scale)
        qkv_refs[g][0] = w
        return w

    def k_item(g):
        z = proj(QKV_W + g * GROUP_W)
        w = _pack_pair(rope(z[:, lo]), rope(z[:, hi]))
        qkv_refs[N_GROUPS + g][0] = w
        return w

    def v_item(g):
        z = proj(2 * QKV_W + g * GROUP_W)
        w = _pack_pair(z[:, lo], z[:, hi])
        qkv_refs[2 * N_GROUPS + g][0] = w
        return w

    def gate_item(ref, c0, c):
        s = _sigmoid(proj(c0 + c * PRE_CH))
        ref[0, :, c * PRE_CH:(c + 1) * PRE_CH] = s
        return s

    items = []
    for g in range(N_GROUPS):
        items += [functools.partial(q_item, g), functools.partial(k_item, g),
                  functools.partial(v_item, g)]
    for c in range(D_MODEL // PRE_CH):
        items += [functools.partial(gate_item, ga_ref, g0, c),
                  functools.partial(gate_item, yc_ref, g0 + D_MODEL, c)]
    n_chunks = D_MODEL // LANES

    def between(c):
        zero = jnp.zeros((1, LANES), F32)
        for item in items[c * len(items) // n_chunks:(c + 1) * len(items) // n_chunks]:
            zero = zero + _zero_from(item())
        return zero

    tail_zeros = _causal_conv(hbuf, cwb_ref, cb_ref, conv_scr, t, between)
    hbuf[0:HALO, :] = hbuf[t:t + HALO, :]

    hv = conv_scr[...]
    mu = jnp.mean(hv, axis=-1, keepdims=True)
    xc = hv - mu
    var = jnp.mean(xc * xc, axis=-1, keepdims=True)
    ln_b = lnb_ref[...] + jnp.tile(tail_zeros[0] + tail_zeros[1], (1, D_MODEL // LANES))
    hn = xc * lax.rsqrt(var + EPS) * lng_ref[...] + ln_b
    hn = hn * _sigmoid(hn)
    y_conv = jnp.dot(hn.astype(BF16), cwo_ref[...], preferred_element_type=F32)
    yc_ref[0] = yc_ref[0] * y_conv


def _pre(x, mod, norm_g, w_in, cos, sin, conv_wb, conv_b, ln_g, ln_b, conv_wo, l):
    b, s, d = x.shape
    t = PRE_T
    tok = lambda w: pl.BlockSpec((1, t, w), lambda i, j: (i, j, 0))
    n_qkv = 3 * N_GROUPS
    outs = pl.pallas_call(
        _pre_kernel,
        grid=(b, s // t),
        in_specs=[
            tok(d),
            pl.BlockSpec((1, N_MOD, d), lambda i, j: (i, 0, 0)),
            _const_spec((1, d), (l * 3 + 1,)),
            _const_spec((d, N_IN), (l,)),
            pl.BlockSpec((t, LANES), lambda i, j: (j, 0)),
            pl.BlockSpec((t, LANES), lambda i, j: (j, 0)),
            _const_spec((CONV_K * SUBLANES, d), (l,)),
            _const_spec((1, d), (l,)), _const_spec((1, d), (l,)), _const_spec((1, d), (l,)),
            _const_spec((d, d), (l,)),
        ],
        out_specs=[tok(LANES)] * n_qkv + [tok(d), tok(d)],
        out_shape=[jax.ShapeDtypeStruct((b, s, LANES), jnp.uint32)] * n_qkv + [
            jax.ShapeDtypeStruct((b, s, d), F32),
            jax.ShapeDtypeStruct((b, s, d), F32),
        ],
        scratch_shapes=[pltpu.VMEM((HALO + t + SUBLANES, d), F32), pltpu.VMEM((t, d), F32)],
        compiler_params=_params(2),
        name="pre",
    )(x, mod, norm_g, w_in, cos, sin, conv_wb, conv_b, ln_g, ln_b, conv_wo)
    return outs[:n_qkv], outs[n_qkv], outs[n_qkv + 1]


ATTN_T = 2048
N_PAIRS = GROUP_W // (2 * HEAD_DIM)
ATTN_UNROLL = 8


def _unroll(trips):
    return max(u for u in range(1, ATTN_UNROLL + 1) if trips % u == 0)


def _rows(start, dil):
    if dil == 1:
        return pl.ds(start, BLK)
    return pl.ds(start, BLK, stride=dil)


def _attn_kernel(*refs):
    g_refs = [refs[5 * g:5 * g + 5] for g in range(N_GROUPS)]
    o_ref, m_scr, a_scr, s_scr, bias_scr = refs[5 * N_GROUPS:]
    n = pl.program_id(1)
    qi = lax.broadcasted_iota(jnp.int32, (BLK, 2 * BLK), 0)
    kj = lax.broadcasted_iota(jnp.int32, (BLK, 2 * BLK), 1)
    band = (kj >= qi) & (kj <= qi + BLK)
    band_first = band & (kj >= jnp.where(n == 0, BLK, 0))
    bias_scr[0] = jnp.where(band, 0.0, NEG_INF)
    bias_scr[1] = jnp.where(band_first, 0.0, NEG_INF)
    band, band_first = 0, 1
    lane = lax.broadcasted_iota(jnp.int32, (BLK, LANES), 1)
    head0 = lane < HEAD_DIM
    zero = jnp.zeros((BLK, LANES), BF16)

    def heads(qw, kw, vw, mask):
        res = []
        for pair in range(N_PAIRS):
            qp, kp, vp = (_unpack_pair(w, pair) for w in (qw, kw, vw))
            outs, lses = [], []
            for hh in range(2):
                sel = head0 if hh == 0 else jnp.logical_not(head0)
                qh = jnp.where(sel, qp, zero)
                sc = lax.dot_general(qh, kp, (((1,), (1,)), ((), ())),
                                     preferred_element_type=F32)
                sc = sc + bias_scr[mask]
                mx = jnp.max(sc, axis=-1, keepdims=True)
                p = jnp.exp(sc - mx)
                den = jnp.sum(p, axis=-1, keepdims=True)
                oh = jnp.dot(p.astype(BF16), vp, preferred_element_type=F32)
                outs.append(oh / den)
                lses.append(jnp.broadcast_to(mx + jnp.log(den), (BLK, LANES)))
            res.append((jnp.where(head0, outs[0], outs[1]), jnp.where(head0, lses[0], lses[1])))
        return res

    def merge(pos, rows, res):
        for pair, (o, lse) in enumerate(res):
            if pos == 0:
                m_scr[pair, rows, :] = lse
                a_scr[pair, rows, :] = o
                continue
            m_old = m_scr[pair, rows, :]
            m_new = jnp.maximum(m_old, lse)
            e_old, e_new = jnp.exp(m_old - m_new), jnp.exp(lse - m_new)
            acc = a_scr[pair, rows, :] * e_old + o * e_new
            if pos < N_GROUPS - 1:
                m_scr[pair, rows, :] = m_new
                a_scr[pair, rows, :] = acc
                if pos == 1:
                    s_scr[pair, rows, :] = e_old + e_new
                else:
                    s_scr[pair, rows, :] = s_scr[pair, rows, :] * e_old + e_new
            else:
                o_ref[0, pair, rows, :] = acc / (s_scr[pair, rows, :] * e_old + e_new)

    order = sorted(range(N_GROUPS), key=lambda g: -ATTN_GROUPS[g][1])
    for pos, g in enumerate(order):
        dil = ATTN_GROUPS[g][1]
        win = BLK * dil
        q2, k2, v2, kh2, vh2 = (r.at[0] for r in g_refs[g])

        def block(start, k_prev, v_prev, mask, pos=pos, dil=dil, q2=q2, k2=k2, v2=v2):
            rows = _rows(start, dil)
            kw = jnp.concatenate([k_prev, k2[rows, :]], axis=0)
            vw = jnp.concatenate([v_prev, v2[rows, :]], axis=0)
            merge(pos, rows, heads(q2[rows, :], kw, vw, mask))

        def first(r, carry, dil=dil, kh2=kh2, vh2=vh2, block=block):
            rows = _rows(r, dil)
            block(r, kh2[rows, :], vh2[rows, :], band_first)
            return carry

        def later(idx, carry, dil=dil, win=win, k2=k2, v2=v2, block=block):
            shift = dil.bit_length() - 1
            start = (lax.shift_right_logical(idx, shift) + 1) * win + (idx & (dil - 1))
            if dil == 1:
                start = pl.multiple_of(start, BLK)
            prev = _rows(start - win, dil)
            block(start, k2[prev, :], v2[prev, :], band)
            return carry

        n_later = (ATTN_T // win - 1) * dil
        lax.fori_loop(0, dil, first, 0, unroll=_unroll(dil))
        if n_later:
            lax.fori_loop(0, n_later, later, 0, unroll=_unroll(n_later))


def _attn(qkv):
    b, s, _ = qkv[0].shape
    t = ATTN_T
    cur = pl.BlockSpec((1, t, LANES), lambda i, n: (i, n, 0))
    args, specs = [], []
    for g, (_, dil) in enumerate(ATTN_GROUPS):
        win = BLK * dil
        per = t // win
        halo = pl.BlockSpec((1, win, LANES),
                            lambda i, n, per=per: (i, jnp.maximum(n * per - 1, 0), 0))
        q, k, v = qkv[g], qkv[N_GROUPS + g], qkv[2 * N_GROUPS + g]
        args += [q, k, v, k, v]
        specs += [cur, cur, cur, halo, halo]
    state = pltpu.VMEM((N_PAIRS, t, LANES), F32)
    return pl.pallas_call(
        _attn_kernel,
        grid=(b, s // t),
        in_specs=specs,
        out_specs=pl.BlockSpec((1, N_PAIRS, t, LANES), lambda i, n: (i, 0, n, 0)),
        out_shape=jax.ShapeDtypeStruct((b, N_PAIRS, s, LANES), F32),
        scratch_shapes=[state, state, state, pltpu.VMEM((2, BLK, 2 * BLK), F32)],
        compiler_params=_params(2),
        name="attn",
    )(*args)


def _rope_tables(s):
    half = HEAD_DIM // 2
    inv_freq = ROPE_THETA ** (-(jnp.arange(half, dtype=F32) * 2.0 / HEAD_DIM))
    ang = jnp.arange(s, dtype=F32)[:, None] * inv_freq[None, :]
    cos, sin = jnp.cos(ang), jnp.sin(ang)
    reps = LANES // HEAD_DIM
    cos_t = jnp.tile(jnp.concatenate([cos, cos], axis=-1), (1, reps))
    sin_t = jnp.tile(jnp.concatenate([-sin, sin], axis=-1), (1, reps))
    return cos_t, sin_t


def kernel(x, c, ada_w, ada_b, norm_g, ffn_wg, ffn_wu, ffn_wd, w_in, attn_wo,
           conv_w, conv_b, conv_ln_g, conv_ln_b, conv_wo, w_out, final_g):
    b, s, d = x.shape
    cos_t, sin_t = _rope_tables(s)
    mod_all = _ada(c, ada_w, ada_b)
    rows = lambda v: v.reshape(-1, 1, d)
    norm_g, conv_b, conv_ln_g, conv_ln_b = (rows(v) for v in (norm_g, conv_b, conv_ln_g, conv_ln_b))
    fg = final_g.reshape(1, d)
    wg, wu, wd, w_in, conv_wo, attn_wo, w_out = (
        w.astype(BF16) for w in (ffn_wg, ffn_wu, ffn_wd, w_in, conv_wo, attn_wo, w_out))
    conv_wb = jnp.repeat(conv_w, SUBLANES, axis=1)
    for l in range(DEPTH):
        mod = mod_all[l]
        x = _ffn(x, mod, norm_g, wg, wu, wd, fg, l, 0)
        qkv, ga, yc = _pre(x, mod, norm_g, w_in, cos_t, sin_t, conv_wb, conv_b, conv_ln_g,
                           conv_ln_b, conv_wo, l)
        x = _ffn(x, mod, norm_g, wg, wu, wd, fg, l, 1,
                 mix=(ga, yc, _attn(qkv), attn_wo, w_out), final=(l == DEPTH - 1))
    return x
```

```python
import functools

import jax
import jax.numpy as jnp
from jax import lax
from jax.experimental import pallas as pl
from jax.experimental.pallas import tpu as pltpu

D_MODEL = 1024
DEPTH = 2
HEAD_DIM = 64
HEADS_PER_GROUP = 4
ATTN_GROUPS = ((128, 1), (512, 4), (2048, 16))
N_GROUPS = len(ATTN_GROUPS)
GROUP_W = HEADS_PER_GROUP * HEAD_DIM
QKV_W = N_GROUPS * GROUP_W
CONV_K = 31
D_FF = 2816
ROPE_THETA = 10000.0
EPS = 1e-6
N_MOD = 9
NEG_INF = -1e30
N_IN = 3 * QKV_W + 2 * D_MODEL + 2 * D_MODEL

V7X_LANES = 128
V7X_SUBLANES = 8
V7X_VMEM_BYTES = 64 * 1024 * 1024
LANES, SUBLANES = V7X_LANES, V7X_SUBLANES
VMEM_LIMIT = V7X_VMEM_BYTES - 8 * 1024 * 1024
BLK = ATTN_GROUPS[0][0] // ATTN_GROUPS[0][1]
assert all(w // d == BLK for w, d in ATTN_GROUPS)
HALO = 32
CONV_ROWS = 128

F32 = jnp.float32
BF16 = jnp.bfloat16


def _const_spec(shape, lead=()):
    index = tuple(lead) + (0,) * len(shape)
    return pl.BlockSpec((None,) * len(lead) + tuple(shape), lambda *_: index,
                        pipeline_mode=pl.Buffered(1))


def _params(n_axes):
    return pltpu.CompilerParams(
        dimension_semantics=("arbitrary",) * n_axes, vmem_limit_bytes=VMEM_LIMIT)


def _sigmoid(v):
    return 1.0 / (1.0 + jnp.exp(-v))


def _rms_norm(x, g):
    return (x * lax.rsqrt(jnp.mean(x * x, axis=-1, keepdims=True) + EPS)) * g


def _modulate(x, g, mod_ref, i):
    shift = mod_ref[0, 3 * i:3 * i + 1, :]
    scale = mod_ref[0, 3 * i + 1:3 * i + 2, :]
    return _rms_norm(x, g) * (1.0 + scale) + shift


ADA_TN = D_MODEL

def _ada_kernel(c_ref, w_ref, b_ref, o_ref):
    c = c_ref[...]
    ca = c * _sigmoid(c)
    o_ref[0] = jnp.dot(ca, w_ref[0], preferred_element_type=F32,
                       precision=lax.Precision.HIGHEST) + b_ref[0]


def _ada(c, ada_w, ada_b):
    b = c.shape[0]
    rows = -(-b // SUBLANES) * SUBLANES
    c_pad = jnp.pad(c, ((0, rows - b), (0, 0)))
    n = N_MOD * D_MODEL
    tn = ADA_TN
    out = pl.pallas_call(
        _ada_kernel,
        grid=(DEPTH, n // tn),
        in_specs=[
            pl.BlockSpec((rows, D_MODEL), lambda l, j: (0, 0)),
            pl.BlockSpec((1, D_MODEL, tn), lambda l, j: (l, 0, j)),
            pl.BlockSpec((1, 1, tn), lambda l, j: (l, 0, j)),
        ],
        out_specs=pl.BlockSpec((1, rows, tn), lambda l, j: (l, 0, j)),
        out_shape=jax.ShapeDtypeStruct((DEPTH, rows, n), F32),
        compiler_params=_params(2),
        name="ada",
    )(c_pad, ada_w, ada_b.reshape(DEPTH, 1, n))
    return out[:, :b].reshape(DEPTH, b, N_MOD, D_MODEL)


FFN_T = 512
FFN_CH = 256


def _ffn_kernel(*refs, idx, mix, final):
    if mix:
        ga_ref, yc_ref, oa_ref, awo_ref, wout_ref = refs[:5]
        refs = refs[5:]
    x_ref, mod_ref, g_ref, wg_ref, wu_ref, wd_ref, fg_ref, o_ref, a_scr = refs
    x = x_ref[0]
    if mix:
        o = jnp.concatenate([oa_ref[0, p] for p in range(N_PAIRS)], axis=-1)
        y_attn = jnp.dot(o.astype(BF16), awo_ref[...], preferred_element_type=F32)
        y = ga_ref[0] * y_attn + yc_ref[0]
        x = x + mod_ref[0, 5:6, :] * jnp.dot(y.astype(BF16), wout_ref[...],
                                             preferred_element_type=F32)
    h = _modulate(x, g_ref[...], mod_ref, idx).astype(BF16)
    for c in range(D_FF // FFN_CH):
        sl = slice(c * FFN_CH, (c + 1) * FFN_CH)
        gg = jnp.dot(h, wg_ref[:, sl], preferred_element_type=F32)
        uu = jnp.dot(h, wu_ref[:, sl], preferred_element_type=F32)
        a_scr[:, sl] = ((gg * _sigmoid(gg)) * uu).astype(BF16)
    y = jnp.dot(a_scr[...], wd_ref[...], preferred_element_type=F32)
    gate = mod_ref[0, 3 * idx + 2:3 * idx + 3, :]
    out = x + (0.5 * gate) * y
    if final:
        out = _rms_norm(out, fg_ref[...])
    o_ref[0] = out


def _ffn(x, mod, norm_g, wg, wu, wd, fg, l, half, *, mix=None, final=False):
    b, s, d = x.shape
    t = FFN_T
    idx = 2 * half
    tok = pl.BlockSpec((1, t, d), lambda i, j: (i, j, 0))
    args, specs = [], []
    if mix is not None:
        args += list(mix)
        specs += [tok, tok, pl.BlockSpec((1, N_PAIRS, t, LANES), lambda i, j: (i, 0, j, 0)),
                  _const_spec((GROUP_W, d), (l,)), _const_spec((d, d), (l,))]
    args += [x, mod, norm_g, wg, wu, wd, fg]
    specs += [tok, pl.BlockSpec((1, N_MOD, d), lambda i, j: (i, 0, 0)),
              _const_spec((1, d), (l * 3 + idx,)),
              _const_spec((d, D_FF), (l, half)), _const_spec((d, D_FF), (l, half)),
              _const_spec((D_FF, d), (l, half)), _const_spec((1, d))]
    return pl.pallas_call(
        functools.partial(_ffn_kernel, idx=idx, mix=mix is not None, final=final),
        grid=(b, s // t),
        in_specs=specs,
        out_specs=tok,
        out_shape=jax.ShapeDtypeStruct((b, s, d), F32),
        scratch_shapes=[pltpu.VMEM((t, D_FF), BF16)],
        compiler_params=_params(2),
        name="ffn_mix" if mix is not None else "ffn",
    )(*args)


PRE_T = 512
PRE_CH = 256
DEINT_DILS = (16,)


def _deint_pitch(dil):
    return PRE_T // dil + SUBLANES


def _deint_rows(dil):
    return dil * _deint_pitch(dil)


def _causal_conv(hbuf, cwb_ref, cb_ref, conv_scr, t, between):
    first = HALO - (CONV_K - 1)
    n_a = (first + CONV_K - 1) // SUBLANES + 1
    n = CONV_ROWS // SUBLANES + 1
    zeros = []
    for c in range(D_MODEL // LANES):
        cols = slice(c * LANES, (c + 1) * LANES)
        wv = [cwb_ref[SUBLANES * j:SUBLANES * (j + 1), cols][None] for j in range(CONV_K)]
        bias = cb_ref[:, cols]
        if c >= 2:
            bias = bias + zeros[c - 2]
        for rb in range(t // CONV_ROWS):
            r0 = rb * CONV_ROWS
            hs = [hbuf[r0 + SUBLANES * a:r0 + SUBLANES * a + CONV_ROWS + SUBLANES, cols]
                  .reshape(n, SUBLANES, LANES) for a in range(n_a)]
            acc = None
            for b in range(SUBLANES):
                part = None
                for a in range(n_a):
                    j = SUBLANES * a + b - first
                    if 0 <= j < CONV_K:
                        term = hs[a] * wv[j]
                        part = term if part is None else part + term
                part = part.reshape(CONV_ROWS + SUBLANES, LANES)[b:b + CONV_ROWS]
                acc = part if acc is None else acc + part
            conv_scr[r0:r0 + CONV_ROWS, cols] = acc + bias
        zeros.append(between(c))
    return zeros[-2:]


def _zero_from(v):
    tile = v[0:SUBLANES, 0:LANES]
    bits = tile if tile.dtype == jnp.uint32 else pltpu.bitcast(tile, jnp.uint32)
    return pltpu.bitcast((bits >> 16) >> 16, F32)[0:1, :]


def _pack_pair(a, b):
    a_bits = pltpu.bitcast(a.astype(BF16).astype(F32), jnp.uint32)
    b_bits = pltpu.bitcast(b.astype(BF16).astype(F32), jnp.uint32)
    return a_bits | (b_bits >> 16)


def _unpack_pair(w, which):
    bits = (w & jnp.uint32(0xFFFF0000)) if which == 0 else (w << 16)
    return pltpu.bitcast(bits, F32).astype(BF16)


def _pre_kernel(x_ref, mod_ref, g_ref, w_ref, cos_ref, sin_ref,
                cwb_ref, cb_ref, lng_ref, lnb_ref, cwo_ref, *rest):
    qkv_refs = rest[:3 * N_GROUPS]
    ga_ref, yc_ref, hbuf, conv_scr = rest[3 * N_GROUPS:]
    j = pl.program_id(1)
    t = PRE_T

    @pl.when(j == 0)
    def _():
        hbuf[0:HALO, :] = jnp.zeros((HALO, D_MODEL), F32)
        hbuf[HALO + t:, :] = jnp.zeros((SUBLANES, D_MODEL), F32)

    x = x_ref[0]
    h = _modulate(x, g_ref[...], mod_ref, 1).astype(BF16)
    cos = cos_ref[...]
    sin = sin_ref[...]
    lane = lax.broadcasted_iota(jnp.int32, (t, LANES), 1)
    low_half = (lane % HEAD_DIM) < (HEAD_DIM // 2)

    def proj(c0):
        return jnp.dot(h, w_ref[:, c0:c0 + PRE_CH], preferred_element_type=F32)

    u0 = 3 * QKV_W
    g0 = u0 + 2 * D_MODEL
    for c in range(D_MODEL // PRE_CH):
        za = proj(u0 + c * PRE_CH)
        zg = proj(u0 + D_MODEL + c * PRE_CH)
        hbuf[HALO:HALO + t, c * PRE_CH:(c + 1) * PRE_CH] = za * _sigmoid(zg)

    def rope(z):
        rot = jnp.where(low_half,
                        pltpu.roll(z, LANES - HEAD_DIM // 2, 1),
                        pltpu.roll(z, HEAD_DIM // 2, 1))
        return z * cos + rot * sin

    lo, hi = slice(0, LANES), slice(LANES, 2 * LANES)
    scale = HEAD_DIM ** -0.5

    def store_words(ref, g, w):
        dil = ATTN_GROUPS[g][1]
        if dil not in DEINT_DILS:
            ref[0] = w
            return w
        pitch = _deint_pitch(dil)
        for k in range(t // SUBLANES):
            first = (SUBLANES * k) % dil * pitch + (SUBLANES * k) // dil
            ref[0, 0, pl.ds(first, SUBLANES, stride=pitch), :] = w[SUBLANES * k:SUBLANES * (k + 1)]
        for r in range(dil):
            ref[0, 0, r * pitch + t // dil:(r + 1) * pitch, :] = jnp.zeros(
                (pitch - t // dil, LANES), jnp.uint32)
        return w

    def q_item(g):
        z = proj(g * GROUP_W)
        return store_words(qkv_refs[g], g,
                           _pack_pair(rope(z[:, lo]) * scale, rope(z[:, hi]) * scale))

    def k_item(g):
        z = proj(QKV_W + g * GROUP_W)
        return store_words(qkv_refs[N_GROUPS + g], g, _pack_pair(rope(z[:, lo]), rope(z[:, hi])))

    def v_item(g):
        z = proj(2 * QKV_W + g * GROUP_W)
        return store_words(qkv_refs[2 * N_GROUPS + g], g, _pack_pair(z[:, lo], z[:, hi]))

    def gate_item(ref, c0, c):
        s = _sigmoid(proj(c0 + c * PRE_CH))
        ref[0, :, c * PRE_CH:(c + 1) * PRE_CH] = s
        return s

    items = []
    for g in range(N_GROUPS):
        items += [functools.partial(q_item, g), functools.partial(k_item, g),
                  functools.partial(v_item, g)]
    for c in range(D_MODEL // PRE_CH):
        items += [functools.partial(gate_item, ga_ref, g0, c),
                  functools.partial(gate_item, yc_ref, g0 + D_MODEL, c)]
    n_chunks = D_MODEL // LANES

    def between(c):
        zero = jnp.zeros((1, LANES), F32)
        for item in items[c * len(items) // n_chunks:(c + 1) * len(items) // n_chunks]:
            zero = zero + _zero_from(item())
        return zero

    tail_zeros = _causal_conv(hbuf, cwb_ref, cb_ref, conv_scr, t, between)
    hbuf[0:HALO, :] = hbuf[t:t + HALO, :]

    hv = conv_scr[...]
    mu = jnp.mean(hv, axis=-1, keepdims=True)
    xc = hv - mu
    var = jnp.mean(xc * xc, axis=-1, keepdims=True)
    ln_b = lnb_ref[...] + jnp.tile(tail_zeros[0] + tail_zeros[1], (1, D_MODEL // LANES))
    hn = xc * lax.rsqrt(var + EPS) * lng_ref[...] + ln_b
    hn = hn * _sigmoid(hn)
    y_conv = jnp.dot(hn.astype(BF16), cwo_ref[...], preferred_element_type=F32)
    yc_ref[0] = yc_ref[0] * y_conv


def _pre(x, mod, norm_g, w_in, cos, sin, conv_wb, conv_b, ln_g, ln_b, conv_wo, l):
    b, s, d = x.shape
    t = PRE_T
    tok = lambda w: pl.BlockSpec((1, t, w), lambda i, j: (i, j, 0))
    n_qkv = 3 * N_GROUPS

    def qkv_shape(g):
        dil = ATTN_GROUPS[g][1]
        shape = (b, s // t, _deint_rows(dil), LANES) if dil in DEINT_DILS else (b, s, LANES)
        return jax.ShapeDtypeStruct(shape, jnp.uint32)

    def qkv_spec(g):
        dil = ATTN_GROUPS[g][1]
        if dil in DEINT_DILS:
            return pl.BlockSpec((1, 1, _deint_rows(dil), LANES), lambda i, j: (i, j, 0, 0))
        return tok(LANES)

    outs = pl.pallas_call(
        _pre_kernel,
        grid=(b, s // t),
        in_specs=[
            tok(d),
            pl.BlockSpec((1, N_MOD, d), lambda i, j: (i, 0, 0)),
            _const_spec((1, d), (l * 3 + 1,)),
            _const_spec((d, N_IN), (l,)),
            pl.BlockSpec((t, LANES), lambda i, j: (j, 0)),
            pl.BlockSpec((t, LANES), lambda i, j: (j, 0)),
            _const_spec((CONV_K * SUBLANES, d), (l,)),
            _const_spec((1, d), (l,)), _const_spec((1, d), (l,)), _const_spec((1, d), (l,)),
            _const_spec((d, d), (l,)),
        ],
        out_specs=[qkv_spec(g) for g in range(N_GROUPS)] * 3 + [tok(d), tok(d)],
        out_shape=[qkv_shape(g) for g in range(N_GROUPS)] * 3 + [
            jax.ShapeDtypeStruct((b, s, d), F32),
            jax.ShapeDtypeStruct((b, s, d), F32),
        ],
        scratch_shapes=[pltpu.VMEM((HALO + t + SUBLANES, d), F32), pltpu.VMEM((t, d), F32)],
        compiler_params=_params(2),
        name="pre",
    )(x, mod, norm_g, w_in, cos, sin, conv_wb, conv_b, ln_g, ln_b, conv_wo)
    return outs[:n_qkv], outs[n_qkv], outs[n_qkv + 1]


ATTN_T = 2048
N_PAIRS = GROUP_W // (2 * HEAD_DIM)
ATTN_UNROLL = 8


def _unroll(trips):
    return max(u for u in range(1, ATTN_UNROLL + 1) if trips % u == 0)


def _rows(start, dil):
    if dil == 1:
        return pl.ds(start, BLK)
    return pl.ds(start, BLK, stride=dil)


def _attn_kernel(*refs):
    g_refs = [refs[5 * g:5 * g + 5] for g in range(N_GROUPS)]
    o_ref, m_scr, a_scr, s_scr, bias_scr = refs[5 * N_GROUPS:]
    n = pl.program_id(1)
    qi = lax.broadcasted_iota(jnp.int32, (BLK, 2 * BLK), 0)
    kj = lax.broadcasted_iota(jnp.int32, (BLK, 2 * BLK), 1)
    in_band = (kj >= qi) & (kj <= qi + BLK)
    in_band_first = in_band & (kj >= jnp.where(n == 0, BLK, 0))
    band, band_first = 0, 1
    bias_scr[band] = jnp.where(in_band, 0.0, NEG_INF)
    bias_scr[band_first] = jnp.where(in_band_first, 0.0, NEG_INF)
    lane = lax.broadcasted_iota(jnp.int32, (BLK, LANES), 1)
    head0 = lane < HEAD_DIM
    zero = jnp.zeros((BLK, LANES), BF16)

    def heads(qw, kw, vw, mask):
        res = []
        for pair in range(N_PAIRS):
            qp, kp, vp = (_unpack_pair(w, pair) for w in (qw, kw, vw))
            outs, lses = [], []
            for hh in range(2):
                sel = head0 if hh == 0 else jnp.logical_not(head0)
                qh = jnp.where(sel, qp, zero)
                sc = lax.dot_general(qh, kp, (((1,), (1,)), ((), ())),
                                     preferred_element_type=F32)
                sc = sc + bias_scr[mask]
                mx = jnp.max(sc, axis=-1, keepdims=True)
                p = jnp.exp(sc - mx)
                den = jnp.sum(p, axis=-1, keepdims=True)
                oh = jnp.dot(p.astype(BF16), vp, preferred_element_type=F32)
                outs.append(oh / den)
                lses.append(jnp.broadcast_to(mx + jnp.log(den), (BLK, LANES)))
            res.append((jnp.where(head0, outs[0], outs[1]), jnp.where(head0, lses[0], lses[1])))
        return res

    def merge(pos, rows, res):
        for pair, (o, lse) in enumerate(res):
            if pos == 0:
                m_scr[pair, rows, :] = lse
                a_scr[pair, rows, :] = o
                continue
            m_old = m_scr[pair, rows, :]
            m_new = jnp.maximum(m_old, lse)
            e_old, e_new = jnp.exp(m_old - m_new), jnp.exp(lse - m_new)
            acc = a_scr[pair, rows, :] * e_old + o * e_new
            if pos < N_GROUPS - 1:
                m_scr[pair, rows, :] = m_new
                a_scr[pair, rows, :] = acc
                if pos == 1:
                    s_scr[pair, rows, :] = e_old + e_new
                else:
                    s_scr[pair, rows, :] = s_scr[pair, rows, :] * e_old + e_new
            else:
                o_ref[0, pair, rows, :] = acc / (s_scr[pair, rows, :] * e_old + e_new)

    order = sorted(range(N_GROUPS), key=lambda g: -ATTN_GROUPS[g][1])
    for pos, g in enumerate(order):
        dil = ATTN_GROUPS[g][1]
        win = BLK * dil
        q_ref, k_ref, v_ref, kh_ref, vh_ref = g_refs[g]

        def words(ref, start, dil=dil):
            if dil not in DEINT_DILS:
                return ref[0, _rows(start, dil), :]
            assert BLK * dil == ATTN_T
            r0 = pl.multiple_of(start * _deint_pitch(dil), SUBLANES)
            return jnp.concatenate([ref[0, p, pl.ds(r0, PRE_T // dil), :]
                                    for p in range(ATTN_T // PRE_T)], axis=0)

        def block(start, k_prev, v_prev, mask, pos=pos, dil=dil, words=words,
                  q_ref=q_ref, k_ref=k_ref, v_ref=v_ref):
            kw = jnp.concatenate([k_prev, words(k_ref, start)], axis=0)
            vw = jnp.concatenate([v_prev, words(v_ref, start)], axis=0)
            merge(pos, _rows(start, dil), heads(words(q_ref, start), kw, vw, mask))

        def first(r, carry, words=words, kh_ref=kh_ref, vh_ref=vh_ref, block=block):
            block(r, words(kh_ref, r), words(vh_ref, r), band_first)
            return carry

        def later(idx, carry, dil=dil, win=win, words=words, k_ref=k_ref, v_ref=v_ref,
                  block=block):
            shift = dil.bit_length() - 1
            start = (lax.shift_right_logical(idx, shift) + 1) * win + (idx & (dil - 1))
            if dil == 1:
                start = pl.multiple_of(start, BLK)
            block(start, words(k_ref, start - win), words(v_ref, start - win), band)
            return carry

        n_later = (ATTN_T // win - 1) * dil
        lax.fori_loop(0, dil, first, 0, unroll=_unroll(dil))
        if n_later:
            lax.fori_loop(0, n_later, later, 0, unroll=_unroll(n_later))


def _attn(qkv):
    b, s = qkv[0].shape[:2]
    t = ATTN_T
    args, specs = [], []
    for g, (_, dil) in enumerate(ATTN_GROUPS):
        win = BLK * dil
        per = t // win
        if dil in DEINT_DILS:
            blk = (1, t // PRE_T, _deint_rows(dil), LANES)
            cur = pl.BlockSpec(blk, lambda i, n: (i, n, 0, 0))
            halo = pl.BlockSpec(blk, lambda i, n: (i, jnp.maximum(n - 1, 0), 0, 0))
        else:
            cur = pl.BlockSpec((1, t, LANES), lambda i, n: (i, n, 0))
            halo = pl.BlockSpec((1, win, LANES),
                                lambda i, n, per=per: (i, jnp.maximum(n * per - 1, 0), 0))
        q, k, v = qkv[g], qkv[N_GROUPS + g], qkv[2 * N_GROUPS + g]
        args += [q, k, v, k, v]
        specs += [cur, cur, cur, halo, halo]
    state = pltpu.VMEM((N_PAIRS, t, LANES), F32)
    return pl.pallas_call(
        _attn_kernel,
        grid=(b, s // t),
        in_specs=specs,
        out_specs=pl.BlockSpec((1, N_PAIRS, t, LANES), lambda i, n: (i, 0, n, 0)),
        out_shape=jax.ShapeDtypeStruct((b, N_PAIRS, s, LANES), F32),
        scratch_shapes=[state, state, state, pltpu.VMEM((2, BLK, 2 * BLK), F32)],
        compiler_params=_params(2),
        name="attn",
    )(*args)


def _rope_tables(s):
    half = HEAD_DIM // 2
    inv_freq = ROPE_THETA ** (-(jnp.arange(half, dtype=F32) * 2.0 / HEAD_DIM))
    ang = jnp.arange(s, dtype=F32)[:, None] * inv_freq[None, :]
    cos, sin = jnp.cos(ang), jnp.sin(ang)
    reps = LANES // HEAD_DIM
    cos_t = jnp.tile(jnp.concatenate([cos, cos], axis=-1), (1, reps))
    sin_t = jnp.tile(jnp.concatenate([-sin, sin], axis=-1), (1, reps))
    return cos_t, sin_t


def kernel(x, c, ada_w, ada_b, norm_g, ffn_wg, ffn_wu, ffn_wd, w_in, attn_wo,
           conv_w, conv_b, conv_ln_g, conv_ln_b, conv_wo, w_out, final_g):
    b, s, d = x.shape
    cos_t, sin_t = _rope_tables(s)
    mod_all = _ada(c, ada_w, ada_b)
    rows = lambda v: v.reshape(-1, 1, d)
    norm_g, conv_b, conv_ln_g, conv_ln_b = (rows(v) for v in (norm_g, conv_b, conv_ln_g, conv_ln_b))
    fg = final_g.reshape(1, d)
    wg, wu, wd, w_in, conv_wo, attn_wo, w_out = (
        w.astype(BF16) for w in (ffn_wg, ffn_wu, ffn_wd, w_in, conv_wo, attn_wo, w_out))
    conv_wb = jnp.repeat(conv_w, SUBLANES, axis=1)
    for l in range(DEPTH):
        mod = mod_all[l]
        x = _ffn(x, mod, norm_g, wg, wu, wd, fg, l, 0)
        qkv, ga, yc = _pre(x, mod, norm_g, w_in, cos_t, sin_t, conv_wb, conv_b, conv_ln_g,
                           conv_ln_b, conv_wo, l)
        x = _ffn(x, mod, norm_g, wg, wu, wd, fg, l, 1,
                 mix=(ga, yc, _attn(qkv), attn_wo, w_out), final=(l == DEPTH - 1))
    return x
```
